```python
import math
import jax, jax.numpy as jnp
from jax import lax
import numpy as np

D_MODEL = 2048
BATCH = 4
SEQ = 2048
DEPTH = 4
DEC_BATCH = 8
DEC_SEQ = 1
PAST_LEN = 16384
PAGE_SIZE = 128

N_MIXERS = 3
HD_A = 128
H_A = D_MODEL // (2 * HD_A)
EXPAND_B = 128
H_B = D_MODEL // EXPAND_B
DK_B = EXPAND_B
DV_B = D_MODEL // H_B
CHUNK_B = 64
HD_C = 128
H_C = D_MODEL // HD_C
D_FF = ((8 * D_MODEL + 3 * 256 - 1) // (3 * 256)) * 256
Q_BLOCK = 128
EPS = 1e-6
N_A = (DEPTH + 2) // 3
N_B = (DEPTH + 1) // 3
N_C = DEPTH // 3

kernel_name = "hybrid_diffattn_hgrn2_fox_step"


def rmsnorm(x, g):
    xf = x.astype(jnp.float32)
    y = xf * lax.rsqrt(jnp.mean(xf * xf, axis=-1, keepdims=True) + EPS)
    return (y * g.astype(jnp.float32)).astype(x.dtype)


def alibi_slopes(n_heads):
    return jnp.exp2(-8.0 * (jnp.arange(n_heads, dtype=jnp.float32) + 1.0) / n_heads)


def gather_pages(cache, layer, page_table):
    g = cache[layer, page_table]
    b, n, p = g.shape[:3]
    return g.reshape((b, n * p) + g.shape[3:])


def seg_scores(q, k_segs):
    return jnp.concatenate(
        [jnp.einsum("bqhd,bkhd->bhqk", q, k).astype(jnp.float32) for k in k_segs], axis=-1)


def seg_values(p, v_segs):
    outs = []
    off = 0
    for v in v_segs:
        n = v.shape[1]
        outs.append(jnp.einsum("bhqk,bkhe->bqhe", p[..., off:off + n].astype(v.dtype), v))
        off += n
    return sum(outs[1:], outs[0])


def masked_softmax(logits, mask):
    return jax.nn.softmax(jnp.where(mask, logits, -jnp.inf), axis=-1)


def query_block_sweep(block_fn, q, qpos):
    b, n = q.shape[:2]

    def body(start):
        qb = lax.dynamic_slice_in_dim(q, start, Q_BLOCK, axis=1)
        pb = lax.dynamic_slice_in_dim(qpos, start, Q_BLOCK, axis=0)
        return block_fn(qb, pb)

    out = lax.map(body, jnp.arange(n // Q_BLOCK, dtype=jnp.int32) * Q_BLOCK)
    out = jnp.moveaxis(out, 0, 1)
    return out.reshape((b, n) + out.shape[3:])


def swiglu(h, w_gu, w_down):
    a, u = jnp.split(h @ w_gu, 2, axis=-1)
    return (jax.nn.silu(a) * u) @ w_down


def diff_attention(hp, hs, cache_k, cache_v, layer, page_table, w_in, q_norm, k_norm,
                   lq1, lk1, lq2, lk2, subln_g, w_out, layer_idx):
    lam_init = 0.8 - 0.6 * math.exp(-0.3 * layer_idx)
    f32 = jnp.float32
    lam = (jnp.exp(jnp.sum(lq1.astype(f32) * lk1.astype(f32)))
           - jnp.exp(jnp.sum(lq2.astype(f32) * lk2.astype(f32))) + lam_init)
    slopes = jnp.repeat(alibi_slopes(H_A), 2)
    scale = HD_A ** -0.5

    def project(h):
        b, t, _ = h.shape
        q, k, v = jnp.split(h @ w_in, [D_MODEL, 2 * D_MODEL], axis=-1)
        q = rmsnorm(q.reshape(b, t, 2 * H_A, HD_A), q_norm)
        k = rmsnorm(k.reshape(b, t, 2 * H_A, HD_A), k_norm)
        return q, k, v.reshape(b, t, H_A, 2 * HD_A)

    def core(q, qpos, k_segs, v_segs, kpos):
        b, tq = q.shape[:2]
        dist = qpos[:, None] - kpos[None, :]
        logits = seg_scores(q, k_segs) * scale - slopes[:, None, None] * dist.astype(f32)
        p = masked_softmax(logits, dist >= 0)
        p = p.reshape(b, H_A, 2, tq, p.shape[-1])
        w = p[:, :, 0] - lam * p[:, :, 1]
        return seg_values(w, v_segs)

    def finish(o):
        b, t = o.shape[:2]
        o = rmsnorm(o, subln_g) * (1.0 - lam_init)
        return o.reshape(b, t, D_MODEL) @ w_out

    qp, kp, vp = project(hp)
    pos_p = jnp.arange(hp.shape[1], dtype=jnp.int32)
    op = query_block_sweep(lambda qb, pb: core(qb, pb, (kp,), (vp,), pos_p), qp, pos_p)

    qs, kss, vs = project(hs)
    k_past = gather_pages(cache_k, layer, page_table)
    v_past = gather_pages(cache_v, layer, page_table)
    past, ds = k_past.shape[1], hs.shape[1]
    kpos = jnp.arange(past + ds, dtype=jnp.int32)
    qpos = past + jnp.arange(ds, dtype=jnp.int32)
    os_ = core(qs, qpos, (k_past, kss), (v_past, vs), kpos)
    return finish(op), finish(os_), kp, vp, kss, vs


def gla_chunk(S, qkvg):
    q, k, v, logg = qkvg
    c = q.shape[1]
    G = jnp.cumsum(logg, axis=1)
    inter = jnp.einsum("bthk,bhkv->bthv", q * jnp.exp(G), S)
    causal = jnp.tril(jnp.ones((c, c), dtype=bool))[None, :, :, None, None]
    diff = G[:, :, None] - G[:, None, :]
    decay = jnp.where(causal, jnp.exp(jnp.where(causal, diff, 0.0)), 0.0)
    A = jnp.einsum("bthk,bshk,btshk->bhts", q, k, decay)
    o = inter + jnp.einsum("bhts,bshv->bthv", A, v)
    G_last = G[:, -1]
    S_new = (jnp.exp(G_last)[..., None] * S
             + jnp.einsum("bshk,bshv->bhkv", k * jnp.exp(G_last[:, None] - G), v))
    return S_new, o


def hgrn2(hp, hs, state, w_in, lb, g_norm, w_out):
    f32 = jnp.float32
    scale = DK_B ** -0.5
    lb = lb.reshape(H_B, DK_B)

    def project(h):
        b, t, _ = h.shape
        q, f, i, g = jnp.split((h @ w_in).astype(f32), 4, axis=-1)
        shp = (b, t, H_B, DK_B)
        q = jax.nn.silu(q).reshape(shp) * scale
        gate = lb + (1.0 - lb) * jax.nn.sigmoid(f.reshape(shp))
        return (q, 1.0 - gate, i.reshape(b, t, H_B, DV_B), jnp.log(gate),
                g.reshape(b, t, H_B, DV_B))

    def finish(o, g, dtype):
        b, t = o.shape[:2]
        o = rmsnorm(o, g_norm) * jax.nn.silu(g)
        return o.reshape(b, t, D_MODEL).astype(dtype) @ w_out

    q, k, v, logg, g = project(hp)
    b, t = hp.shape[:2]

    def chunks(a):
        return jnp.moveaxis(a.reshape((b, t // CHUNK_B, CHUNK_B) + a.shape[2:]), 1, 0)

    s0 = jnp.zeros((b, H_B, DK_B, DV_B), f32)
    s_p, o = lax.scan(gla_chunk, s0, (chunks(q), chunks(k), chunks(v), chunks(logg)))
    o = jnp.moveaxis(o, 0, 1).reshape(b, t, H_B, DV_B)
    y_p = finish(o, g, hp.dtype)

    qs, kss, vs, lgs, gs = project(hs)
    s_s, o_s = gla_chunk(state.astype(f32), (qs, kss, vs, lgs))
    return y_p, finish(o_s, gs, hs.dtype), s_p, s_s


def forgetting_attention(hp, hs, cache_k, cache_v, cache_logf, layer, page_table,
                         w_in, b_f, q_norm, k_norm, w_out):
    f32 = jnp.float32
    scale = HD_C ** -0.5

    def project(h):
        b, t, _ = h.shape
        q, k, v, fz = jnp.split(h @ w_in, [D_MODEL, 2 * D_MODEL, 3 * D_MODEL], axis=-1)
        shp = (b, t, H_C, HD_C)
        q = rmsnorm(q.reshape(shp), q_norm)
        k = rmsnorm(k.reshape(shp), k_norm)
        logf = jax.nn.log_sigmoid(fz.astype(f32) + b_f.astype(f32))
        return q, k, v.reshape(shp), logf

    def core(q, qpos, k_segs, v_segs, kpos, F):
        Fq = jnp.take(F, qpos, axis=1)
        bias = jnp.swapaxes(Fq, 1, 2)[..., :, None] - jnp.swapaxes(F, 1, 2)[..., None, :]
        mask = qpos[:, None] >= kpos[None, :]
        p = masked_softmax(seg_scores(q, k_segs) * scale + bias, mask)
        return seg_values(p, v_segs)

    def finish(o):
        b, t = o.shape[:2]
        return o.reshape(b, t, D_MODEL) @ w_out

    qp, kp, vp, lfp = project(hp)
    pos_p = jnp.arange(hp.shape[1], dtype=jnp.int32)
    Fp = jnp.cumsum(lfp, axis=1)
    op = query_block_sweep(lambda qb, pb: core(qb, pb, (kp,), (vp,), pos_p, Fp), qp, pos_p)

    qs, kss, vs, lfs = project(hs)
    k_past = gather_pages(cache_k, layer, page_table)
    v_past = gather_pages(cache_v, layer, page_table)
    lf_past = gather_pages(cache_logf, layer, page_table).astype(f32)
    past, ds = k_past.shape[1], hs.shape[1]
    Fs = jnp.cumsum(jnp.concatenate([lf_past, lfs], axis=1), axis=1)
    kpos = jnp.arange(past + ds, dtype=jnp.int32)
    qpos = past + jnp.arange(ds, dtype=jnp.int32)
    os_ = core(qs, qpos, (k_past, kss), (v_past, vs), kpos, Fs)
    return finish(op), finish(os_), kp, vp, lfp, kss, vs, lfs


def setup_inputs(seed: int = 0) -> dict:
    key = jax.random.key(seed)
    ks = iter(jax.random.split(key, 40))

    def nrm(shape, s=1.0):
        return jax.random.normal(next(ks), shape, jnp.float32) * s

    n_pages = PAST_LEN // PAGE_SIZE
    n_phys = (DEC_BATCH * n_pages * 5 + 3) // 4
    wd = D_MODEL ** -0.5
    x_prompt = nrm((BATCH, SEQ, D_MODEL))
    x_sample = nrm((DEC_BATCH, DEC_SEQ, D_MODEL))
    cache_k_a = nrm((N_A, n_phys, PAGE_SIZE, 2 * H_A, HD_A))
    cache_v_a = nrm((N_A, n_phys, PAGE_SIZE, H_A, 2 * HD_A))
    state_s_b = nrm((N_B, DEC_BATCH, H_B, DK_B, DV_B), 0.5)
    cache_k_c = nrm((N_C, n_phys, PAGE_SIZE, H_C, HD_C))
    cache_v_c = nrm((N_C, n_phys, PAGE_SIZE, H_C, HD_C))
    cache_logf_c = jax.nn.log_sigmoid(1.0 + nrm((N_C, n_phys, PAGE_SIZE, H_C)))
    page_table = jax.random.permutation(next(ks), n_phys)[: DEC_BATCH * n_pages]
    page_table = page_table.reshape(DEC_BATCH, n_pages).astype(jnp.int32)
    return {
        "x_prompt": x_prompt,
        "x_sample": x_sample,
        "cache_k_a": cache_k_a,
        "cache_v_a": cache_v_a,
        "state_s_b": state_s_b,
        "cache_k_c": cache_k_c,
        "cache_v_c": cache_v_c,
        "cache_logf_c": cache_logf_c,
        "page_table": page_table,
        "norm1_g": 1.0 + nrm((DEPTH, D_MODEL), 0.02),
        "norm2_g": 1.0 + nrm((DEPTH, D_MODEL), 0.02),
        "a_w_in": nrm((N_A, D_MODEL, 3 * D_MODEL), wd),
        "a_q_norm": 1.0 + nrm((N_A, HD_A), 0.02),
        "a_k_norm": 1.0 + nrm((N_A, HD_A), 0.02),
        "a_lam_q1": nrm((N_A, HD_A), 0.1),
        "a_lam_k1": nrm((N_A, HD_A), 0.1),
        "a_lam_q2": nrm((N_A, HD_A), 0.1),
        "a_lam_k2": nrm((N_A, HD_A), 0.1),
        "a_subln_g": 1.0 + nrm((N_A, 2 * HD_A), 0.02),
        "a_w_out": nrm((N_A, D_MODEL, D_MODEL), wd),
        "b_w_in": nrm((N_B, D_MODEL, 4 * D_MODEL), wd),
        "b_lower_bounds": nrm((DEPTH, D_MODEL), 0.1),
        "b_out_norm_g": 1.0 + nrm((N_B, DV_B), 0.02),
        "b_w_out": nrm((N_B, D_MODEL, D_MODEL), wd),
        "c_w_in": nrm((N_C, D_MODEL, 3 * D_MODEL + H_C), wd),
        "c_b_f": 1.0 + nrm((N_C, H_C), 0.1),
        "c_q_norm": 1.0 + nrm((N_C, HD_C), 0.02),
        "c_k_norm": 1.0 + nrm((N_C, HD_C), 0.02),
        "c_w_out": nrm((N_C, D_MODEL, D_MODEL), wd),
        "ffn_w_gu": nrm((DEPTH, D_MODEL, 2 * D_FF), wd),
        "ffn_w_down": nrm((DEPTH, D_FF, D_MODEL), D_FF ** -0.5),
    }


def reference(x_prompt, x_sample, cache_k_a, cache_v_a, state_s_b, cache_k_c, cache_v_c,
              cache_logf_c, page_table, norm1_g, norm2_g, a_w_in, a_q_norm, a_k_norm,
              a_lam_q1, a_lam_k1, a_lam_q2, a_lam_k2, a_subln_g, a_w_out, b_w_in,
              b_lower_bounds, b_out_norm_g, b_w_out, c_w_in, c_b_f, c_q_norm, c_k_norm,
              c_w_out, ffn_w_gu, ffn_w_down):
    p_lb = jax.nn.softmax(b_lower_bounds.astype(jnp.float32), axis=0)
    lb_all = jnp.cumsum(p_lb, axis=0) - p_lb[0]

    xp, xs = x_prompt, x_sample
    ka_p, va_p, ka_s, va_s = [], [], [], []
    sb_p, sb_s = [], []
    kc_p, vc_p, lc_p, kc_s, vc_s, lc_s = [], [], [], [], [], []
    for i in range(DEPTH):
        j = i // N_MIXERS
        hp = rmsnorm(xp, norm1_g[i])
        hs = rmsnorm(xs, norm1_g[i])
        kind = i % N_MIXERS
        if kind == 0:
            yp, ys, k1, v1, k2, v2 = diff_attention(
                hp, hs, cache_k_a, cache_v_a, j, page_table, a_w_in[j], a_q_norm[j], a_k_norm[j],
                a_lam_q1[j], a_lam_k1[j], a_lam_q2[j], a_lam_k2[j], a_subln_g[j], a_w_out[j], i)
            ka_p.append(k1); va_p.append(v1); ka_s.append(k2); va_s.append(v2)
        elif kind == 1:
            yp, ys, s1, s2 = hgrn2(hp, hs, state_s_b[j], b_w_in[j], lb_all[i],
                                   b_out_norm_g[j], b_w_out[j])
            sb_p.append(s1); sb_s.append(s2)
        else:
            yp, ys, k1, v1, l1, k2, v2, l2 = forgetting_attention(
                hp, hs, cache_k_c, cache_v_c, cache_logf_c, j, page_table, c_w_in[j], c_b_f[j],
                c_q_norm[j], c_k_norm[j], c_w_out[j])
            kc_p.append(k1); vc_p.append(v1); lc_p.append(l1)
            kc_s.append(k2); vc_s.append(v2); lc_s.append(l2)
        xp = xp + yp
        xs = xs + ys
        xp = xp + swiglu(rmsnorm(xp, norm2_g[i]), ffn_w_gu[i], ffn_w_down[i])
        xs = xs + swiglu(rmsnorm(xs, norm2_g[i]), ffn_w_gu[i], ffn_w_down[i])

    k_a_prompt = jnp.stack(ka_p); v_a_prompt = jnp.stack(va_p)
    k_a_sample = jnp.stack(ka_s); v_a_sample = jnp.stack(va_s)
    s_b_prompt = jnp.stack(sb_p); s_b_sample = jnp.stack(sb_s)
    k_c_prompt = jnp.stack(kc_p); v_c_prompt = jnp.stack(vc_p); logf_c_prompt = jnp.stack(lc_p)
    k_c_sample = jnp.stack(kc_s); v_c_sample = jnp.stack(vc_s); logf_c_sample = jnp.stack(lc_s)
    return (xp, xs, k_a_prompt, v_a_prompt, k_a_sample, v_a_sample, s_b_prompt, s_b_sample,
            k_c_prompt, v_c_prompt, logf_c_prompt, k_c_sample, v_c_sample, logf_c_sample)
```

```python
import functools
import math

import numpy as np
import jax
import jax.numpy as jnp
from jax import lax
from jax.experimental import pallas as pl
from jax.experimental.pallas import tpu as pltpu

F32 = jnp.float32
BF16 = jnp.bfloat16
EPS = 1e-6
LANES = 128
SUBLANES = 8
VMEM_LIMIT = 56 * 1024 * 1024
HEAD = 128
GLA_CHUNK = 128
NEG_INF = float("-inf")


def _cp(n_axes):
    return pltpu.CompilerParams(dimension_semantics=("arbitrary",) * n_axes,
                                vmem_limit_bytes=VMEM_LIMIT)


def _sigmoid(x):
    return 1.0 / (1.0 + jnp.exp(-x))


def _silu(x):
    return x * _sigmoid(x)


def _log_sigmoid(x):
    return jnp.minimum(x, 0.0) - jnp.log1p(jnp.exp(-jnp.abs(x)))


def _nt_dot(a, b):
    return lax.dot_general(a, b, (((1,), (1,)), ((), ())), preferred_element_type=F32)


def _rmsnorm_kernel(x_ref, g_ref, o_ref, *, post_scale):
    x = x_ref[...].astype(F32)
    ms = jnp.mean(x * x, axis=-1, keepdims=True)
    y = x * lax.rsqrt(ms + EPS) * g_ref[...]
    if post_scale != 1.0:
        y = y * post_scale
    o_ref[...] = y.astype(o_ref.dtype)


def _rmsnorm(x, g, out_dtype, post_scale=1.0):
    rows, d = x.shape
    tr = min(rows, 512)
    return pl.pallas_call(
        functools.partial(_rmsnorm_kernel, post_scale=post_scale),
        grid=(rows // tr,),
        in_specs=[pl.BlockSpec((tr, d), lambda i: (i, 0)),
                  pl.BlockSpec((1, d), lambda i: (0, 0))],
        out_specs=pl.BlockSpec((tr, d), lambda i: (i, 0)),
        out_shape=jax.ShapeDtypeStruct((rows, d), out_dtype),
        compiler_params=_cp(1),
        name="rmsnorm",
    )(x, g.reshape(1, d).astype(F32))


def _mm_kernel(*refs, mode, n_out, tn):
    dual = mode == "swiglu"
    has_aux = mode in ("gnorm", "res", "logsig")
    it = iter(refs)
    x_ref = next(it)
    w_ref = next(it)
    w2_ref = next(it) if dual else None
    aux_ref = next(it) if has_aux else None
    out_refs = [next(it) for _ in range(n_out)]
    wbf_ref = next(it)
    wbf2_ref = next(it) if dual else None

    @pl.when(pl.program_id(1) == 0)
    def _():
        wbf_ref[...] = w_ref[...].astype(BF16)
        if dual:
            wbf2_ref[...] = w2_ref[...].astype(BF16)

    x = x_ref[...].astype(BF16)
    acc = jnp.dot(x, wbf_ref[...], preferred_element_type=F32)
    if mode == "swiglu":
        up = jnp.dot(x, wbf2_ref[...], preferred_element_type=F32)
        acc = _silu(acc) * up
    elif mode == "res":
        acc = aux_ref[...] + acc
    elif mode == "logsig":
        acc = _log_sigmoid(acc + aux_ref[...])
    if mode == "gnorm":
        gain = aux_ref[...]
        for c in range(tn // HEAD):
            blk = acc[:, c * HEAD:(c + 1) * HEAD]
            ms = jnp.mean(blk * blk, axis=-1, keepdims=True)
            y = blk * lax.rsqrt(ms + EPS) * gain
            for o_ref in out_refs:
                o_ref[:, c * HEAD:(c + 1) * HEAD] = y.astype(o_ref.dtype)
    else:
        for o_ref in out_refs:
            o_ref[...] = acc.astype(o_ref.dtype)


def _matmul(x, w, *, col0=0, ncols=None, mode="plain", out_dtypes=(F32,), aux=None,
            col1=None, tm=512, tn=512):
    m, k = x.shape
    ncols = w.shape[1] - col0 if ncols is None else ncols
    tm = min(tm, m)
    tn = min(tn, ncols)
    assert m % tm == 0 and ncols % tn == 0 and col0 % tn == 0
    nb0 = col0 // tn
    in_specs = [pl.BlockSpec((tm, k), lambda n, i: (i, 0)),
                pl.BlockSpec((k, tn), lambda n, i: (0, n + nb0))]
    args = [x, w]
    scratch = [pltpu.VMEM((k, tn), BF16)]
    if mode == "swiglu":
        assert col1 % tn == 0
        nb1 = col1 // tn
        in_specs.append(pl.BlockSpec((k, tn), lambda n, i: (0, n + nb1)))
        args.append(w)
        scratch.append(pltpu.VMEM((k, tn), BF16))
    if mode == "gnorm":
        in_specs.append(pl.BlockSpec((1, HEAD), lambda n, i: (0, 0)))
        args.append(aux.reshape(1, HEAD).astype(F32))
    elif mode == "res":
        in_specs.append(pl.BlockSpec((tm, tn), lambda n, i: (i, n)))
        args.append(aux)
    elif mode == "logsig":
        in_specs.append(pl.BlockSpec((1, tn), lambda n, i: (0, n)))
        args.append(aux.reshape(1, ncols).astype(F32))
    outs = pl.pallas_call(
        functools.partial(_mm_kernel, mode=mode, n_out=len(out_dtypes), tn=tn),
        grid=(ncols // tn, m // tm),
        in_specs=in_specs,
        out_specs=[pl.BlockSpec((tm, tn), lambda n, i: (i, n)) for _ in out_dtypes],
        out_shape=[jax.ShapeDtypeStruct((m, ncols), dt) for dt in out_dtypes],
        scratch_shapes=scratch,
        compiler_params=_cp(2),
        name="mm_" + mode,
    )(*args)
    return outs[0] if len(out_dtypes) == 1 else outs


def _lam_value(lq1, lk1, lq2, lk2, lam_init):
    a = jnp.sum(lq1[...] * lk1[...], axis=-1, keepdims=True)
    b = jnp.sum(lq2[...] * lk2[...], axis=-1, keepdims=True)
    return jnp.exp(a) - jnp.exp(b) + lam_init


def _online_softmax_step(s, v, m_ref, l_ref, acc_ref, j):
    m_prev = m_ref[j]
    m_new = jnp.maximum(m_prev, jnp.max(s, axis=1, keepdims=True))
    alpha = jnp.exp(m_prev - m_new)
    p = jnp.exp(s - m_new)
    l_ref[j] = alpha * l_ref[j] + jnp.sum(p, axis=1, keepdims=True)
    acc_ref[j] = alpha * acc_ref[j] + jnp.dot(p.astype(BF16), v, preferred_element_type=F32)
    m_ref[j] = m_new


def _attn_a_kernel(q_ref, k_ref, v_ref, lq1, lk1, lq2, lk2, sg_ref, o_ref,
                   m_ref, l_ref, acc_ref, *, tq, n_heads, lam_init):
    h = pl.program_id(1)
    qi = pl.program_id(2)
    scale = HEAD ** -0.5
    slope = jnp.exp2(-8.0 * (jnp.full((1, 1), h, jnp.int32).astype(F32) + 1.0) / n_heads)
    lam = _lam_value(lq1, lk1, lq2, lk2, lam_init)
    col = lax.broadcasted_iota(jnp.int32, (1, tq), 1)
    m_ref[...] = jnp.full(m_ref.shape, NEG_INF, F32)
    l_ref[...] = jnp.zeros(l_ref.shape, F32)
    acc_ref[...] = jnp.zeros(acc_ref.shape, F32)
    q = q_ref[0]

    def step(ki, masked):
        start = pl.multiple_of(ki * tq, tq)
        k = k_ref[0, pl.ds(start, tq), :]
        v = v_ref[0, pl.ds(start, tq), :]
        bias = slope * (col + (ki - qi) * tq).astype(F32)
        for j in range(2):
            s = _nt_dot(q[:, j * HEAD:(j + 1) * HEAD], k[:, j * HEAD:(j + 1) * HEAD]) * scale + bias
            if masked:
                row = lax.broadcasted_iota(jnp.int32, (tq, tq), 0)
                cc = lax.broadcasted_iota(jnp.int32, (tq, tq), 1)
                s = jnp.where(row >= cc, s, NEG_INF)
            _online_softmax_step(s, v, m_ref, l_ref, acc_ref, j)

    def body(ki, carry):
        step(ki, False)
        return carry

    lax.fori_loop(0, qi, body, 0)
    step(qi, True)
    o1 = acc_ref[0] * (1.0 / l_ref[0])
    o2 = acc_ref[1] * (1.0 / l_ref[1])
    o = o1 - lam * o2
    ms = jnp.mean(o * o, axis=-1, keepdims=True)
    y = o * lax.rsqrt(ms + EPS) * sg_ref[...]
    o_ref[0] = (y * (1.0 - lam_init)).astype(o_ref.dtype)


def _attn_a_prompt(q, k, v, lq1, lk1, lq2, lk2, subln_g, lam_init, tq=256):
    b, t, d = q.shape
    n_heads = d // (2 * HEAD)
    tq = min(tq, t)
    vec = lambda a: a.reshape(1, -1).astype(F32)
    small = lambda w: pl.BlockSpec((1, w), lambda bi, h, qi: (0, 0))
    return pl.pallas_call(
        functools.partial(_attn_a_kernel, tq=tq, n_heads=n_heads, lam_init=lam_init),
        grid=(b, n_heads, t // tq),
        in_specs=[pl.BlockSpec((1, tq, 2 * HEAD), lambda bi, h, qi: (bi, qi, h)),
                  pl.BlockSpec((1, t, 2 * HEAD), lambda bi, h, qi: (bi, 0, h)),
                  pl.BlockSpec((1, t, 2 * HEAD), lambda bi, h, qi: (bi, 0, h)),
                  small(HEAD), small(HEAD), small(HEAD), small(HEAD), small(2 * HEAD)],
        out_specs=pl.BlockSpec((1, tq, 2 * HEAD), lambda bi, h, qi: (bi, qi, h)),
        out_shape=jax.ShapeDtypeStruct((b, t, d), BF16),
        scratch_shapes=[pltpu.VMEM((2, tq, 1), F32), pltpu.VMEM((2, tq, 1), F32),
                        pltpu.VMEM((2, tq, 2 * HEAD), F32)],
        compiler_params=_cp(3),
        name="attn_a_prompt",
    )(q, k, v, vec(lq1), vec(lk1), vec(lq2), vec(lk2), vec(subln_g))


def _attn_c_kernel(q_ref, k_ref, v_ref, f_ref, o_ref, m_ref, l_ref, acc_ref, *, tq):
    qi = pl.program_id(2)
    scale = HEAD ** -0.5
    m_ref[...] = jnp.full(m_ref.shape, NEG_INF, F32)
    l_ref[...] = jnp.zeros(l_ref.shape, F32)
    acc_ref[...] = jnp.zeros(acc_ref.shape, F32)
    q = q_ref[0]
    q_start = pl.multiple_of(qi * tq, tq)
    f_first = f_ref[0, 0, :, pl.ds(q_start, tq)][:, 0:1]

    def step(ki, masked):
        start = pl.multiple_of(ki * tq, tq)
        k = k_ref[0, pl.ds(start, tq), :]
        v = v_ref[0, pl.ds(start, tq), :]
        bias = f_first - f_ref[0, 0, :, pl.ds(start, tq)]
        s = _nt_dot(q, k) * scale + bias
        if masked:
            row = lax.broadcasted_iota(jnp.int32, (tq, tq), 0)
            cc = lax.broadcasted_iota(jnp.int32, (tq, tq), 1)
            s = jnp.where(row >= cc, s, NEG_INF)
        _online_softmax_step(s, v, m_ref, l_ref, acc_ref, 0)

    def body(ki, carry):
        step(ki, False)
        return carry

    lax.fori_loop(0, qi, body, 0)
    step(qi, True)
    o_ref[0] = (acc_ref[0] * (1.0 / l_ref[0])).astype(o_ref.dtype)


def _attn_c_prompt(q, k, v, f_t, tq=256):
    b, t, d = q.shape
    n_heads = d // HEAD
    tq = min(tq, t)
    return pl.pallas_call(
        functools.partial(_attn_c_kernel, tq=tq),
        grid=(b, n_heads, t // tq),
        in_specs=[pl.BlockSpec((1, tq, HEAD), lambda bi, h, qi: (bi, qi, h)),
                  pl.BlockSpec((1, t, HEAD), lambda bi, h, qi: (bi, 0, h)),
                  pl.BlockSpec((1, t, HEAD), lambda bi, h, qi: (bi, 0, h)),
                  pl.BlockSpec((1, 1, 1, t), lambda bi, h, qi: (bi, h, 0, 0))],
        out_specs=pl.BlockSpec((1, tq, HEAD), lambda bi, h, qi: (bi, qi, h)),
        out_shape=jax.ShapeDtypeStruct((b, t, d), BF16),
        scratch_shapes=[pltpu.VMEM((1, tq, 1), F32), pltpu.VMEM((1, tq, 1), F32),
                        pltpu.VMEM((1, tq, HEAD), F32)],
        compiler_params=_cp(3),
        name="attn_c_prompt",
    )(q, k, v, f_t)


def _cumsum_rows(x, n):
    row = lax.broadcasted_iota(jnp.int32, x.shape, 0)
    d = 1
    while d < n:
        x = x + jnp.where(row >= d, pltpu.roll(x, d, 0), 0.0)
        d *= 2
    return x


def _logf_cumsum_kernel(x_ref, o_ref, carry_ref, *, tc, n_heads):
    @pl.when(pl.program_id(1) == 0)
    def _():
        carry_ref[...] = jnp.zeros(carry_ref.shape, F32)

    f = _cumsum_rows(x_ref[0], tc) + carry_ref[...]
    carry_ref[...] = f[tc - 1:tc, :]
    o_ref[0] = f.T[0:n_heads, :]


def _logf_cumsum_t(logf_pad, n_heads, tc=128):
    b, t, _ = logf_pad.shape
    tc = min(tc, t)
    return pl.pallas_call(
        functools.partial(_logf_cumsum_kernel, tc=tc, n_heads=n_heads),
        grid=(b, t // tc),
        in_specs=[pl.BlockSpec((1, tc, LANES), lambda bi, c: (bi, c, 0))],
        out_specs=pl.BlockSpec((1, n_heads, tc), lambda bi, c: (bi, 0, c)),
        out_shape=jax.ShapeDtypeStruct((b, n_heads, t), F32),
        scratch_shapes=[pltpu.VMEM((1, LANES), F32)],
        compiler_params=_cp(2),
        name="logf_cumsum",
    )(logf_pad)


def _gla_levels():
    return [GLA_CHUNK >> (i + 1) for i in range(int(math.log2(GLA_CHUNK)))]


def _gla_masks():
    t = np.arange(GLA_CHUNK)[:, None]
    s = np.arange(GLA_CHUNK)[None, :]
    masks = []
    for b in _gla_levels():
        masks.append((t // (2 * b) == s // (2 * b)) & (t % (2 * b) >= b) & (s % (2 * b) < b))
    masks.append(t == s)
    return jnp.asarray(np.stack(masks).astype(np.float32))


def _level_reference(g_ref, b):
    row = lambda r, n: jnp.broadcast_to(g_ref[r:r + 1, :], (n, LANES))
    if 2 * b >= SUBLANES:
        return jnp.concatenate([row(2 * b * j + b - 1, 2 * b) for j in range(GLA_CHUNK // (2 * b))], axis=0)
    sub = lax.broadcasted_iota(jnp.int32, (SUBLANES, LANES), 0)
    pieces = []
    for j in range(GLA_CHUNK // SUBLANES):
        base = SUBLANES * j
        n_blk = SUBLANES // (2 * b)
        piece = row(base + (n_blk - 1) * 2 * b + b - 1, SUBLANES)
        for i in range(n_blk - 2, -1, -1):
            piece = jnp.where(sub < (i + 1) * 2 * b, row(base + i * 2 * b + b - 1, SUBLANES), piece)
        pieces.append(piece)
    return jnp.concatenate(pieces, axis=0)


def _lower_bound(lb_ref, layer_idx):
    x = lb_ref[...].astype(F32)
    e = jnp.exp(x - jnp.max(x, axis=0, keepdims=True))
    p = e / jnp.sum(e, axis=0, keepdims=True)
    r = lax.broadcasted_iota(jnp.int32, p.shape, 0)
    return jnp.sum(jnp.where((r >= 1) & (r <= layer_idx), p, 0.0), axis=0, keepdims=True)


def _gla_kernel(q_ref, f_ref, i_ref, g_ref, lb_ref, gn_ref, mask_ref, o_ref, s_ref,
                st_ref, gs_ref, *, layer_idx, n_chunks):
    c = pl.program_id(2)
    scale = HEAD ** -0.5

    @pl.when(c == 0)
    def _():
        st_ref[...] = jnp.zeros(st_ref.shape, F32)

    lb = _lower_bound(lb_ref, layer_idx)
    q = _silu(q_ref[...]) * scale
    gate = lb + (1.0 - lb) * _sigmoid(f_ref[...])
    kk = 1.0 - gate
    v = i_ref[...]
    v_bf = v.astype(BF16)
    g_cum = _cumsum_rows(jnp.log(gate), GLA_CHUNK)
    gs_ref[...] = g_cum
    st = st_ref[...]
    o = _nt_dot((q * jnp.exp(g_cum)).astype(BF16), st.astype(BF16))

    levels = _gla_levels()
    a = jnp.sum(q * kk, axis=1, keepdims=True) * mask_ref[len(levels)]
    for lvl, b in enumerate(levels):
        d = g_cum - _level_reference(gs_ref, b)
        qs = (q * jnp.exp(jnp.minimum(d, 0.0))).astype(BF16)
        ks = (kk * jnp.exp(jnp.minimum(-d, 0.0))).astype(BF16)
        a = a + _nt_dot(qs, ks) * mask_ref[lvl]
    o = o + jnp.dot(a.astype(BF16), v_bf, preferred_element_type=F32)

    g_last = gs_ref[GLA_CHUNK - 1:GLA_CHUNK, :]
    k_dec = (kk * jnp.exp(g_last - g_cum)).astype(BF16)
    st_new = st * jnp.exp(g_last) + jnp.dot(v.T.astype(BF16), k_dec, preferred_element_type=F32)
    st_ref[...] = st_new

    ms = jnp.mean(o * o, axis=-1, keepdims=True)
    y = o * lax.rsqrt(ms + EPS) * gn_ref[...]
    o_ref[...] = (y * _silu(g_ref[...])).astype(o_ref.dtype)

    @pl.when(c == n_chunks - 1)
    def _():
        s_ref[...] = st_new.T


def _gla_prompt(proj, lower_bounds, g_norm, layer_idx, b, t):
    m, d4 = proj.shape
    d = d4 // 4
    n_heads = d // HEAD
    depth = lower_bounds.shape[0]
    n_chunks = t // GLA_CHUNK
    masks = _gla_masks()
    blk = lambda off: pl.BlockSpec((GLA_CHUNK, HEAD), lambda bi, h, c: (bi * n_chunks + c, off * n_heads + h))
    return pl.pallas_call(
        functools.partial(_gla_kernel, layer_idx=layer_idx, n_chunks=n_chunks),
        grid=(b, n_heads, n_chunks),
        in_specs=[blk(0), blk(1), blk(2), blk(3),
                  pl.BlockSpec((depth, HEAD), lambda bi, h, c: (0, h)),
                  pl.BlockSpec((1, HEAD), lambda bi, h, c: (0, 0)),
                  pl.BlockSpec(masks.shape, lambda bi, h, c: (0, 0, 0))],
        out_specs=[pl.BlockSpec((GLA_CHUNK, HEAD), lambda bi, h, c: (bi * n_chunks + c, h)),
                   pl.BlockSpec((None, None, HEAD, HEAD), lambda bi, h, c: (bi, h, 0, 0))],
        out_shape=[jax.ShapeDtypeStruct((m, d), BF16),
                   jax.ShapeDtypeStruct((b, n_heads, HEAD, HEAD), F32)],
        scratch_shapes=[pltpu.VMEM((HEAD, HEAD), F32), pltpu.VMEM((GLA_CHUNK, HEAD), F32)],
        compiler_params=_cp(3),
        name="gla_prompt",
    )(proj, proj, proj, proj, lower_bounds, g_norm.reshape(1, HEAD).astype(F32), masks)


def _gla_step_kernel(i_ref, g_ref, qt_ref, ft_ref, lbt_ref, gn_ref, s_in_ref, o_ref, s_out_ref,
                     *, layer_idx, n_batch):
    scale = HEAD ** -0.5
    xt = lbt_ref[...].astype(F32)
    et = jnp.exp(xt - jnp.max(xt, axis=1, keepdims=True))
    pt = et / jnp.sum(et, axis=1, keepdims=True)
    ct = lax.broadcasted_iota(jnp.int32, pt.shape, 1)
    lb_col = jnp.sum(jnp.where((ct >= 1) & (ct <= layer_idx), pt, 0.0), axis=1, keepdims=True)

    q_col = _silu(qt_ref[...]) * scale
    gate_col = lb_col + (1.0 - lb_col) * _sigmoid(ft_ref[...])
    k_col = 1.0 - gate_col
    a_all = jnp.sum(q_col * k_col, axis=0, keepdims=True)
    qg_col = q_col * gate_col
    for bi in range(n_batch):
        s_old = s_in_ref[bi]
        v_row = i_ref[bi:bi + 1, :]
        o = (jnp.sum(qg_col[:, bi:bi + 1] * s_old, axis=0, keepdims=True)
             + a_all[:, bi:bi + 1] * v_row)
        s_out_ref[bi] = gate_col[:, bi:bi + 1] * s_old + k_col[:, bi:bi + 1] * v_row
        ms = jnp.mean(o * o, axis=-1, keepdims=True)
        y = o * lax.rsqrt(ms + EPS) * gn_ref[...]
        o_ref[bi:bi + 1, :] = y * _silu(g_ref[bi:bi + 1, :])


def _gla_step(proj, state, layer_j, lower_bounds, g_norm, layer_idx):
    nb, d4 = proj.shape
    d = d4 // 4
    n_heads = d // HEAD
    depth = lower_bounds.shape[0]
    proj_t = proj.T
    lb_t = lower_bounds.T
    row = lambda off: pl.BlockSpec((nb, HEAD), lambda h: (0, off * n_heads + h))
    col = lambda off: pl.BlockSpec((HEAD, nb), lambda h: (off * n_heads + h, 0))
    return pl.pallas_call(
        functools.partial(_gla_step_kernel, layer_idx=layer_idx, n_batch=nb),
        grid=(n_heads,),
        in_specs=[row(2), row(3), col(0), col(1),
                  pl.BlockSpec((HEAD, depth), lambda h: (h, 0)),
                  pl.BlockSpec((1, HEAD), lambda h: (0, 0)),
                  pl.BlockSpec((None, nb, None, HEAD, HEAD), lambda h: (layer_j, 0, h, 0, 0))],
        out_specs=[pl.BlockSpec((nb, HEAD), lambda h: (0, h)),
                   pl.BlockSpec((nb, None, HEAD, HEAD), lambda h: (0, h, 0, 0))],
        out_shape=[jax.ShapeDtypeStruct((nb, d), F32),
                   jax.ShapeDtypeStruct((nb, n_heads, HEAD, HEAD), F32)],
        compiler_params=_cp(1),
        name="gla_step",
    )(proj, proj, proj_t, proj_t, lb_t, g_norm.reshape(1, HEAD).astype(F32), state)


def _decode_kernel(pt_ref, *refs, kind, n_pages, page, n_rows, lam_init):
    it = iter(refs)
    q_ref = next(it)
    kn_ref = next(it)
    vn_ref = next(it)
    kc_ref = next(it)
    vc_ref = next(it)
    if kind == "c":
        lfc_ref = next(it)
        lfn_ref = next(it)
    else:
        lq1, lk1, lq2, lk2 = next(it), next(it), next(it), next(it)
    o_ref = next(it)
    qbf_ref, mask_ref, m_ref, l_ref, acc_ref = next(it), next(it), next(it), next(it), next(it)
    if kind == "c":
        carry_ref = next(it)
    else:
        pair_ref = next(it)

    p = pl.program_id(1)
    scale = HEAD ** -0.5
    group = n_rows if kind == "c" else n_rows // 2
    ncols = page * group
    rows_col = lax.broadcasted_iota(jnp.int32, (n_rows, 1), 0)

    @pl.when(p == 0)
    def _():
        qbf_ref[...] = q_ref[0].astype(BF16)
        r = lax.broadcasted_iota(jnp.int32, (n_rows, ncols), 0)
        cidx = lax.broadcasted_iota(jnp.int32, (n_rows, ncols), 1)
        head_of_row = r if kind == "c" else r // 2
        mask_ref[...] = jnp.where(cidx % group == head_of_row, 0.0, NEG_INF)
        m_ref[...] = jnp.full(m_ref.shape, NEG_INF, F32)
        l_ref[...] = jnp.zeros(l_ref.shape, F32)
        acc_ref[...] = jnp.zeros(acc_ref.shape, F32)
        if kind == "c":
            carry_ref[...] = lfn_ref[0]

    page_idx = n_pages - 1 - p
    lane = lax.broadcasted_iota(jnp.int32, (1, ncols), 1)
    qbf = qbf_ref[...]
    if kind == "a":
        s_even = _nt_dot(qbf, kc_ref[pl.ds(0, ncols, stride=2), :].astype(BF16))
        s_odd = _nt_dot(qbf, kc_ref[pl.ds(1, ncols, stride=2), :].astype(BF16))
        s = jnp.where(rows_col % 2 == 0, s_even, s_odd) * scale
        slope = jnp.exp2(-8.0 * ((rows_col // 2).astype(F32) + 1.0) / group)
        dist = (n_pages * page - (page_idx * page + lane // group)).astype(F32)
        s = s - slope * dist + mask_ref[...]
        v2d = vc_ref[...].reshape(ncols, vc_ref.shape[-1])
    else:
        s = _nt_dot(qbf, kc_ref[...].astype(BF16)) * scale
        lf2d = lfc_ref[...]
        flat = jnp.concatenate([lf2d[i:i + 1, :] for i in range(lf2d.shape[0])], axis=1)
        y = flat
        tot = flat
        dd = group
        while dd < ncols:
            y = y + jnp.where(lane + dd < ncols, pltpu.roll(y, ncols - dd, 1), 0.0)
            tot = tot + pltpu.roll(tot, dd, 1)
            dd *= 2
        carry = carry_ref[...]
        s = s + (y - flat + carry) + mask_ref[...]
        carry_ref[...] = carry + tot
        v2d = vc_ref[...]

    m_prev = m_ref[...]
    m_new = jnp.maximum(m_prev, jnp.max(s, axis=1, keepdims=True))
    alpha = jnp.exp(m_prev - m_new)
    pr = jnp.exp(s - m_new)
    l_ref[...] = alpha * l_ref[...] + jnp.sum(pr, axis=1, keepdims=True)
    acc_ref[...] = alpha * acc_ref[...] + jnp.dot(pr.astype(BF16), v2d.astype(BF16),
                                                  preferred_element_type=F32)
    m_ref[...] = m_new

    @pl.when(p == n_pages - 1)
    def _():
        s_new = jnp.sum(q_ref[0] * kn_ref[0], axis=1, keepdims=True) * scale
        m_prev = m_ref[...]
        m_fin = jnp.maximum(m_prev, s_new)
        alpha = jnp.exp(m_prev - m_fin)
        p_new = jnp.exp(s_new - m_fin)
        l_fin = alpha * l_ref[...] + p_new
        acc = alpha * acc_ref[...] + p_new * vn_ref[0]
        o_full = acc * (1.0 / l_fin)
        if kind == "a":
            lam = _lam_value(lq1, lk1, lq2, lk2, lam_init)
            signed = o_full * jnp.where(rows_col % 2 == 0, 1.0, -lam)
            for c in range(pair_ref.shape[0]):
                pair_ref[c] = signed[:, c * LANES:(c + 1) * LANES]
                o_ref[0, :, c * LANES:(c + 1) * LANES] = (pair_ref[c, pl.ds(0, group, stride=2), :]
                                                          + pair_ref[c, pl.ds(1, group, stride=2), :])
        else:
            o_ref[0] = o_full


def _decode_attention(kind, q, k_new, v_new, cache_k, cache_v, layer_j, page_table, extras, lam_init=0.0):
    nb, n_rows, _ = q.shape
    vw = v_new.shape[-1]
    n_pages = page_table.shape[1]
    page = cache_k.shape[2] // n_rows
    group = n_rows if kind == "c" else n_rows // 2
    ncols = page * group
    rev = lambda p: n_pages - 1 - p
    per_b = lambda shape: pl.BlockSpec((1,) + shape, lambda b, p, pt: (b,) + (0,) * len(shape))
    paged = lambda shape: pl.BlockSpec((None, None) + shape,
                                       lambda b, p, pt: (layer_j, pt[b, rev(p)]) + (0,) * len(shape))
    in_specs = [per_b((n_rows, HEAD)), per_b((n_rows, HEAD)), per_b((n_rows, vw)),
                paged((page * n_rows, HEAD)), paged(cache_v.shape[2:])]
    args = [q, k_new, v_new, cache_k, cache_v]
    scratch = [pltpu.VMEM((n_rows, HEAD), BF16), pltpu.VMEM((n_rows, ncols), F32),
               pltpu.VMEM((n_rows, 1), F32), pltpu.VMEM((n_rows, 1), F32), pltpu.VMEM((n_rows, vw), F32)]
    if kind == "c":
        cache_logf, logf_new = extras
        in_specs += [paged(cache_logf.shape[2:]), per_b((1, ncols))]
        args += [cache_logf, logf_new]
        scratch += [pltpu.VMEM((1, ncols), F32)]
    else:
        in_specs += [pl.BlockSpec((1, HEAD), lambda b, p, pt: (0, 0))] * 4
        args += [e.reshape(1, HEAD).astype(F32) for e in extras]
        scratch += [pltpu.VMEM((vw // LANES, n_rows, LANES), F32)]
    return pl.pallas_call(
        functools.partial(_decode_kernel, kind=kind, n_pages=n_pages, page=page, n_rows=n_rows,
                          lam_init=lam_init),
        grid_spec=pltpu.PrefetchScalarGridSpec(
            num_scalar_prefetch=1,
            grid=(nb, n_pages),
            in_specs=in_specs,
            out_specs=pl.BlockSpec((1, group, vw), lambda b, p, pt: (b, 0, 0)),
            scratch_shapes=scratch),
        out_shape=jax.ShapeDtypeStruct((nb, group, vw), F32),
        compiler_params=_cp(2),
        name="decode_" + kind,
    )(page_table, *args)


def _layer_a(hp, hs, xp, xs, cache_k, cache_v, j, page_table, w_in, q_norm, k_norm, lams, subln_g, w_out,
             layer_idx, b, t):
    d = xp.shape[1]
    lam_init = 0.8 - 0.6 * math.exp(-0.3 * layer_idx)
    n_layers, n_phys, page, n_rows, _ = cache_k.shape
    ck = cache_k.reshape(n_layers, n_phys, page * n_rows, HEAD)

    q = _matmul(hp, w_in, col0=0, ncols=d, mode="gnorm", aux=q_norm, out_dtypes=(BF16,))
    k32, k16 = _matmul(hp, w_in, col0=d, ncols=d, mode="gnorm", aux=k_norm, out_dtypes=(F32, BF16))
    v32, v16 = _matmul(hp, w_in, col0=2 * d, ncols=d, out_dtypes=(F32, BF16))
    o = _attn_a_prompt(q.reshape(b, t, d), k16.reshape(b, t, d), v16.reshape(b, t, d), *lams, subln_g, lam_init)
    xp = _matmul(o.reshape(b * t, d), w_out, mode="res", aux=xp)

    nb = hs.shape[0]
    qs = _matmul(hs, w_in, col0=0, ncols=d, mode="gnorm", aux=q_norm)
    ks = _matmul(hs, w_in, col0=d, ncols=d, mode="gnorm", aux=k_norm)
    vs = _matmul(hs, w_in, col0=2 * d, ncols=d)
    v_rows = jnp.repeat(vs.reshape(nb, n_rows // 2, 2 * HEAD), 2, axis=1)
    os_ = _decode_attention("a", qs.reshape(nb, n_rows, HEAD), ks.reshape(nb, n_rows, HEAD), v_rows,
                            ck, cache_v, j, page_table, lams, lam_init)
    os_ = _rmsnorm(os_.reshape(nb * n_rows // 2, 2 * HEAD), subln_g, F32, post_scale=1.0 - lam_init)
    xs = _matmul(os_.reshape(nb, d), w_out, mode="res", aux=xs)
    return xp, xs, k32, v32, ks, vs


def _layer_b(hp, hs, xp, xs, state, j, w_in, lower_bounds, g_norm, w_out, layer_idx, b, t):
    proj = _matmul(hp, w_in)
    o, s_p = _gla_prompt(proj, lower_bounds, g_norm, layer_idx, b, t)
    xp = _matmul(o, w_out, mode="res", aux=xp)

    proj_s = _matmul(hs, w_in)
    o_s, s_s = _gla_step(proj_s, state, j, lower_bounds, g_norm, layer_idx)
    xs = _matmul(o_s, w_out, mode="res", aux=xs)
    return xp, xs, s_p, s_s


def _layer_c(hp, hs, xp, xs, cache_k, cache_v, cache_logf, j, page_table, w_in, b_f, q_norm, k_norm, w_out, b, t):
    d = xp.shape[1]
    n_heads = d // HEAD
    n_layers, n_phys, page = cache_k.shape[:3]
    ck = cache_k.reshape(n_layers, n_phys, page * n_heads, HEAD)
    cv = cache_v.reshape(n_layers, n_phys, page * n_heads, HEAD)
    clf = cache_logf.reshape(n_layers, n_phys, page * n_heads // LANES, LANES)
    w_f =jnp.pad(w_in[:, 3 * d:], ((0, 0), (0, LANES - n_heads)))
    b_pad = jnp.pad(b_f.astype(F32), (0, LANES - n_heads))

    q = _matmul(hp, w_in, col0=0, ncols=d, mode="gnorm", aux=q_norm, out_dtypes=(BF16,))
    k32, k16 = _matmul(hp, w_in, col0=d, ncols=d, mode="gnorm", aux=k_norm, out_dtypes=(F32, BF16))
    v32, v16 = _matmul(hp, w_in, col0=2 * d, ncols=d, out_dtypes=(F32, BF16))
    logf = _matmul(hp, w_f, mode="logsig", aux=b_pad)
    f_t = _logf_cumsum_t(logf.reshape(b, t, LANES), n_heads)
    o = _attn_c_prompt(q.reshape(b, t, d), k16.reshape(b, t, d), v16.reshape(b, t, d),
                       f_t.reshape(b, n_heads, 1, t))
    xp = _matmul(o.reshape(b * t, d), w_out, mode="res", aux=xp)

    nb = hs.shape[0]
    qs = _matmul(hs, w_in, col0=0, ncols=d, mode="gnorm", aux=q_norm)
    ks = _matmul(hs, w_in, col0=d, ncols=d, mode="gnorm", aux=k_norm)
    vs = _matmul(hs, w_in, col0=2 * d, ncols=d)
    lfs = _matmul(hs, w_f, mode="logsig", aux=b_pad)[:, :n_heads]
    lf_new = jnp.tile(lfs, (1, page)).reshape(nb, 1, page * n_heads)
    os_ = _decode_attention("c", qs.reshape(nb, n_heads, HEAD), ks.reshape(nb, n_heads, HEAD),
                            vs.reshape(nb, n_heads, HEAD), ck, cv, j, page_table, (clf, lf_new))
    xs = _matmul(os_.reshape(nb, d), w_out, mode="res", aux=xs)
    return xp, xs, k32, v32, logf[:, :n_heads], ks, vs, lfs


def _ffn(x, g, w_gu, w_down, h_dtype):
    d_ff = w_down.shape[0]
    h = _rmsnorm(x, g, h_dtype)
    act = _matmul(h, w_gu, col0=0, ncols=d_ff, col1=d_ff, mode="swiglu", out_dtypes=(BF16,))
    return _matmul(act, w_down, mode="res", aux=x)


def kernel(x_prompt, x_sample, cache_k_a, cache_v_a, state_s_b, cache_k_c, cache_v_c, cache_logf_c, page_table, norm1_g, norm2_g, a_w_in, a_q_norm, a_k_norm, a_lam_q1, a_lam_k1, a_lam_q2, a_lam_k2, a_subln_g, a_w_out, b_w_in, b_lower_bounds, b_out_norm_g, b_w_out, c_w_in, c_b_f, c_q_norm, c_k_norm, c_w_out, ffn_w_gu, ffn_w_down):
    b, t, d = x_prompt.shape
    nb, ds, _ = x_sample.shape
    assert ds == 1 and d % (2 * HEAD) == 0 and t % GLA_CHUNK == 0
    depth = norm1_g.shape[0]
    xp = x_prompt.reshape(b * t, d)
    xs = x_sample.reshape(nb, d)
    h_a, h_c = d // (2 * HEAD), d // HEAD
    outs = {name: [] for name in ("ka_p", "va_p", "ka_s", "va_s", "sb_p", "sb_s",
                                  "kc_p", "vc_p", "lc_p", "kc_s", "vc_s", "lc_s")}
    for i in range(depth):
        j = i // 3
        hp = _rmsnorm(xp, norm1_g[i], BF16)
        hs = _rmsnorm(xs, norm1_g[i], F32)
        kind = i % 3
        if kind == 0:
            lams = (a_lam_q1[j], a_lam_k1[j], a_lam_q2[j], a_lam_k2[j])
            xp, xs, k1, v1, k2, v2 = _layer_a(hp, hs, xp, xs, cache_k_a, cache_v_a, j, page_table, a_w_in[j],
                                              a_q_norm[j], a_k_norm[j], lams, a_subln_g[j], a_w_out[j], i, b, t)
            outs["ka_p"].append(k1.reshape(b, t, 2 * h_a, HEAD))
            outs["va_p"].append(v1.reshape(b, t, h_a, 2 * HEAD))
            outs["ka_s"].append(k2.reshape(nb, 1, 2 * h_a, HEAD))
            outs["va_s"].append(v2.reshape(nb, 1, h_a, 2 * HEAD))
        elif kind == 1:
            xp, xs, s1, s2 = _layer_b(hp, hs, xp, xs, state_s_b, j, b_w_in[j], b_lower_bounds,
                                      b_out_norm_g[j], b_w_out[j], i, b, t)
            outs["sb_p"].append(s1)
            outs["sb_s"].append(s2)
        else:
            xp, xs, k1, v1, l1, k2, v2, l2 = _layer_c(hp, hs, xp, xs, cache_k_c, cache_v_c, cache_logf_c, j,
                                                      page_table, c_w_in[j], c_b_f[j], c_q_norm[j],
                                                      c_k_norm[j], c_w_out[j], b, t)
            outs["kc_p"].append(k1.reshape(b, t, h_c, HEAD))
            outs["vc_p"].append(v1.reshape(b, t, h_c, HEAD))
            outs["lc_p"].append(l1.reshape(b, t, h_c))
            outs["kc_s"].append(k2.reshape(nb, 1, h_c, HEAD))
            outs["vc_s"].append(v2.reshape(nb, 1, h_c, HEAD))
            outs["lc_s"].append(l2.reshape(nb, 1, h_c))
        xp = _ffn(xp, norm2_g[i], ffn_w_gu[i], ffn_w_down[i], BF16)
        xs = _ffn(xs, norm2_g[i], ffn_w_gu[i], ffn_w_down[i], F32)
    st = lambda name: jnp.stack(outs[name])
    return (xp.reshape(b, t, d), xs.reshape(nb, 1, d), st("ka_p"), st("va_p"), st("ka_s"), st("va_s"),
            st("sb_p"), st("sb_s"), st("kc_p"), st("vc_p"), st("lc_p"), st("kc_s"), st("vc_s"), st("lc_s"))
```

```python
import functools
import math

import numpy as np
import jax
import jax.numpy as jnp
from jax import lax
from jax.experimental import pallas as pl
from jax.experimental.pallas import tpu as pltpu

F32 = jnp.float32
BF16 = jnp.bfloat16
EPS = 1e-6
LANES = 128
SUBLANES = 8
VMEM_LIMIT = 56 * 1024 * 1024
HEAD = 128
GLA_CHUNK = 128
ATT_TQ = 256
ATT_CK = 512
DECODE_PAGES = 4
LOG2E = 1.4426950408889634
NEG_INF = float("-inf")


def _cp(n_axes):
    return pltpu.CompilerParams(dimension_semantics=("arbitrary",) * n_axes,
                                vmem_limit_bytes=VMEM_LIMIT)


def _sigmoid(x):
    return 1.0 / (1.0 + jnp.exp(-x))


def _silu(x):
    return x * _sigmoid(x)


def _log_sigmoid(x):
    return jnp.minimum(x, 0.0) - jnp.log1p(jnp.exp(-jnp.abs(x)))


def _nt_dot(a, b):
    return lax.dot_general(a, b, (((1,), (1,)), ((), ())), preferred_element_type=F32)


def _rmsnorm_kernel(x_ref, g_ref, o_ref, *, post_scale):
    x = x_ref[...].astype(F32)
    ms = jnp.mean(x * x, axis=-1, keepdims=True)
    y = x * lax.rsqrt(ms + EPS) * g_ref[...]
    if post_scale != 1.0:
        y = y * post_scale
    o_ref[...] = y.astype(o_ref.dtype)


def _rmsnorm(x, g, out_dtype, post_scale=1.0):
    rows, d = x.shape
    tr = min(rows, 512)
    return pl.pallas_call(
        functools.partial(_rmsnorm_kernel, post_scale=post_scale),
        grid=(rows // tr,),
        in_specs=[pl.BlockSpec((tr, d), lambda i: (i, 0)),
                  pl.BlockSpec((1, d), lambda i: (0, 0))],
        out_specs=pl.BlockSpec((tr, d), lambda i: (i, 0)),
        out_shape=jax.ShapeDtypeStruct((rows, d), out_dtype),
        compiler_params=_cp(1),
        name="rmsnorm",
    )(x, g.reshape(1, d).astype(F32))


def _mm_kernel(*refs, mode, n_out, tn):
    dual = mode == "swiglu"
    has_aux = mode in ("gnorm", "res", "logsig")
    it = iter(refs)
    x_ref = next(it)
    w_ref = next(it)
    w2_ref = next(it) if dual else None
    aux_ref = next(it) if has_aux else None
    out_refs = [next(it) for _ in range(n_out)]
    wbf_ref = next(it)
    wbf2_ref = next(it) if dual else None

    @pl.when(pl.program_id(1) == 0)
    def _():
        wbf_ref[...] = w_ref[...].astype(BF16)
        if dual:
            wbf2_ref[...] = w2_ref[...].astype(BF16)

    x = x_ref[...].astype(BF16)
    acc = jnp.dot(x, wbf_ref[...], preferred_element_type=F32)
    if mode == "swiglu":
        up = jnp.dot(x, wbf2_ref[...], preferred_element_type=F32)
        acc = _silu(acc) * up
    elif mode == "res":
        acc = aux_ref[...] + acc
    elif mode == "logsig":
        acc = _log_sigmoid(acc + aux_ref[...])
    if mode == "gnorm":
        gain = aux_ref[...]
        for c in range(tn // HEAD):
            blk = acc[:, c * HEAD:(c + 1) * HEAD]
            ms = jnp.mean(blk * blk, axis=-1, keepdims=True)
            y = blk * lax.rsqrt(ms + EPS) * gain
            for o_ref in out_refs:
                o_ref[:, c * HEAD:(c + 1) * HEAD] = y.astype(o_ref.dtype)
    else:
        for o_ref in out_refs:
            o_ref[...] = acc.astype(o_ref.dtype)


def _matmul(x, w, *, layer=0, col0=0, ncols=None, mode="plain", out_dtypes=(F32,), aux=None,
            col1=None, tm=512, tn=1024):
    if w.ndim == 2:
        w = w[None]
    m, k = x.shape
    ncols = w.shape[2] - col0 if ncols is None else ncols
    tm = min(tm, m)
    tn = min(tn, ncols)
    assert m % tm == 0 and ncols % tn == 0 and col0 % tn == 0
    nb0 = col0 // tn
    in_specs = [pl.BlockSpec((tm, k), lambda n, i: (i, 0)),
                pl.BlockSpec((None, k, tn), lambda n, i: (layer, 0, n + nb0))]
    args = [x, w]
    scratch = [pltpu.VMEM((k, tn), BF16)]
    if mode == "swiglu":
        assert col1 % tn == 0
        nb1 = col1 // tn
        in_specs.append(pl.BlockSpec((None, k, tn), lambda n, i: (layer, 0, n + nb1)))
        args.append(w)
        scratch.append(pltpu.VMEM((k, tn), BF16))
    if mode == "gnorm":
        in_specs.append(pl.BlockSpec((1, HEAD), lambda n, i: (0, 0)))
        args.append(aux.reshape(1, HEAD).astype(F32))
    elif mode == "res":
        in_specs.append(pl.BlockSpec((tm, tn), lambda n, i: (i, n)))
        args.append(aux)
    elif mode == "logsig":
        in_specs.append(pl.BlockSpec((1, tn), lambda n, i: (0, n)))
        args.append(aux.reshape(1, ncols).astype(F32))
    outs = pl.pallas_call(
        functools.partial(_mm_kernel, mode=mode, n_out=len(out_dtypes), tn=tn),
        grid=(ncols // tn, m // tm),
        in_specs=in_specs,
        out_specs=[pl.BlockSpec((tm, tn), lambda n, i: (i, n)) for _ in out_dtypes],
        out_shape=[jax.ShapeDtypeStruct((m, ncols), dt) for dt in out_dtypes],
        scratch_shapes=scratch,
        compiler_params=_cp(2),
        name="mm_" + mode,
    )(*args)
    return outs[0] if len(out_dtypes) == 1 else outs


def _lam_value(lq1, lk1, lq2, lk2, lam_init):
    a = jnp.sum(lq1[...] * lk1[...], axis=-1, keepdims=True)
    b = jnp.sum(lq2[...] * lk2[...], axis=-1, keepdims=True)
    return jnp.exp(a) - jnp.exp(b) + lam_init


def _causal_rows(qs, k_ref, v_ref, s_ref, bias_of_chunk, qi, n_chunks, tq, ck):
    cs = (HEAD ** -0.5) * LOG2E
    n_maps = len(qs)
    row = lax.broadcasted_iota(jnp.int32, (tq, ck), 0)
    col = lax.broadcasted_iota(jnp.int32, (tq, ck), 1)
    m_part = [jnp.full((tq, LANES), NEG_INF, F32) for _ in range(n_maps)]
    for c in range(n_chunks):
        k = k_ref[0, c * ck:(c + 1) * ck, :]
        bias = bias_of_chunk(c)
        for j in range(n_maps):
            s = _nt_dot(qs[j], k[:, j * HEAD:(j + 1) * HEAD]) * cs + bias
            if c == n_chunks - 1:
                s = jnp.where(row - col >= c * ck - qi * tq, s, NEG_INF)
            s_ref[j, :, c * ck:(c + 1) * ck] = s
            for g in range(ck // LANES):
                m_part[j] = jnp.maximum(m_part[j], s[:, g * LANES:(g + 1) * LANES])
    outs = []
    for j in range(n_maps):
        m = jnp.max(m_part[j], axis=1, keepdims=True)
        l_part = jnp.zeros((tq, LANES), F32)
        o = None
        for c in range(n_chunks):
            p = jnp.exp2(s_ref[j, :, c * ck:(c + 1) * ck] - m)
            for g in range(ck // LANES):
                l_part = l_part + p[:, g * LANES:(g + 1) * LANES]
            pv = jnp.dot(p.astype(BF16), v_ref[0, c * ck:(c + 1) * ck, :], preferred_element_type=F32)
            o = pv if o is None else o + pv
        outs.append((o, jnp.sum(l_part, axis=1, keepdims=True)))
    return outs


def _for_each_chunk_count(qi, tq, ck, t, fn):
    per = ck // tq
    for n_chunks in range(1, t // ck + 1):
        pl.when(qi // per == n_chunks - 1)(functools.partial(fn, n_chunks))


def _attn_a_kernel(q_ref, k_ref, v_ref, lq1, lk1, lq2, lk2, sg_ref, o_ref, s_ref,
                   *, tq, ck, t, n_heads, lam_init):
    h = pl.program_id(1)
    qi = pl.program_id(2)
    slope2 = LOG2E * jnp.exp2(-8.0 * (jnp.full((1, 1), h, jnp.int32).astype(F32) + 1.0) / n_heads)
    lam = _lam_value(lq1, lk1, lq2, lk2, lam_init)
    col = lax.broadcasted_iota(jnp.int32, (1, ck), 1)
    q = q_ref[0]
    qs = [q[:, j * HEAD:(j + 1) * HEAD] for j in range(2)]

    def run(n_chunks):
        bias = lambda c: slope2 * (col + (c * ck - qi * tq)).astype(F32)
        (o1, l1), (o2, l2) = _causal_rows(qs, k_ref, v_ref, s_ref, bias, qi, n_chunks, tq, ck)
        o = o1 * (1.0 / l1) - lam * (o2 * (1.0 / l2))
        ms = jnp.mean(o * o, axis=-1, keepdims=True)
        y = o * lax.rsqrt(ms + EPS) * sg_ref[...]
        o_ref[0] = (y * (1.0 - lam_init)).astype(o_ref.dtype)

    _for_each_chunk_count(qi, tq, ck, t, run)


def _attn_tiles(t):
    tq = min(ATT_TQ, t)
    ck = min(ATT_CK, t)
    assert t % ck == 0 and ck % tq == 0
    return tq, ck


def _attn_a_prompt(q, k, v, lq1, lk1, lq2, lk2, subln_g, lam_init):
    b, t, d = q.shape
    n_heads = d // (2 * HEAD)
    tq, ck = _attn_tiles(t)
    vec = lambda a: a.reshape(1, -1).astype(F32)
    small = lambda w: pl.BlockSpec((1, w), lambda bi, h, qi: (0, 0))
    return pl.pallas_call(
        functools.partial(_attn_a_kernel, tq=tq, ck=ck, t=t, n_heads=n_heads, lam_init=lam_init),
        grid=(b, n_heads, t // tq),
        in_specs=[pl.BlockSpec((1, tq, 2 * HEAD), lambda bi, h, qi: (bi, qi, h)),
                  pl.BlockSpec((1, t, 2 * HEAD), lambda bi, h, qi: (bi, 0, h)),
                  pl.BlockSpec((1, t, 2 * HEAD), lambda bi, h, qi: (bi, 0, h)),
                  small(HEAD), small(HEAD), small(HEAD), small(HEAD), small(2 * HEAD)],
        out_specs=pl.BlockSpec((1, tq, 2 * HEAD), lambda bi, h, qi: (bi, qi, h)),
        out_shape=jax.ShapeDtypeStruct((b, t, d), BF16),
        scratch_shapes=[pltpu.VMEM((2, tq, t), F32)],
        compiler_params=_cp(3),
        name="attn_a_prompt",
    )(q, k, v, vec(lq1), vec(lk1), vec(lq2), vec(lk2), vec(subln_g))


def _attn_c_kernel(q_ref, k_ref, v_ref, f_ref, o_ref, s_ref, *, tq, ck, t):
    qi = pl.program_id(2)
    q_start = pl.multiple_of(qi * tq, tq)
    f_first = f_ref[0, 0, :, pl.ds(q_start, tq)][:, 0:1]

    def run(n_chunks):
        bias = lambda c: (f_first - f_ref[0, 0, :, c * ck:(c + 1) * ck]) * LOG2E
        ((o, l),) = _causal_rows([q_ref[0]], k_ref, v_ref, s_ref, bias, qi, n_chunks, tq, ck)
        o_ref[0] = (o * (1.0 / l)).astype(o_ref.dtype)

    _for_each_chunk_count(qi, tq, ck, t, run)


def _attn_c_prompt(q, k, v, f_t):
    b, t, d = q.shape
    n_heads = d // HEAD
    tq, ck = _attn_tiles(t)
    return pl.pallas_call(
        functools.partial(_attn_c_kernel, tq=tq, ck=ck, t=t),
        grid=(b, n_heads, t // tq),
        in_specs=[pl.BlockSpec((1, tq, HEAD), lambda bi, h, qi: (bi, qi, h)),
                  pl.BlockSpec((1, t, HEAD), lambda bi, h, qi: (bi, 0, h)),
                  pl.BlockSpec((1, t, HEAD), lambda bi, h, qi: (bi, 0, h)),
                  pl.BlockSpec((1, 1, 1, t), lambda bi, h, qi: (bi, h, 0, 0))],
        out_specs=pl.BlockSpec((1, tq, HEAD), lambda bi, h, qi: (bi, qi, h)),
        out_shape=jax.ShapeDtypeStruct((b, t, d), BF16),
        scratch_shapes=[pltpu.VMEM((1, tq, t), F32)],
        compiler_params=_cp(3),
        name="attn_c_prompt",
    )(q, k, v, f_t)


def _cumsum_rows(x, n):
    row = lax.broadcasted_iota(jnp.int32, x.shape, 0)
    d = 1
    while d < n:
        x = x + jnp.where(row >= d, pltpu.roll(x, d, 0), 0.0)
        d *= 2
    return x


def _logf_cumsum_kernel(x_ref, o_ref, carry_ref, *, tc, n_heads):
    @pl.when(pl.program_id(1) == 0)
    def _():
        carry_ref[...] = jnp.zeros(carry_ref.shape, F32)

    f = _cumsum_rows(x_ref[0], tc) + carry_ref[...]
    carry_ref[...] = f[tc - 1:tc, :]
    o_ref[0] = f.T[0:n_heads, :]


def _logf_cumsum_t(logf_pad, n_heads, tc=128):
    b, t, _ = logf_pad.shape
    tc = min(tc, t)
    return pl.pallas_call(
        functools.partial(_logf_cumsum_kernel, tc=tc, n_heads=n_heads),
        grid=(b, t // tc),
        in_specs=[pl.BlockSpec((1, tc, LANES), lambda bi, c: (bi, c, 0))],
        out_specs=pl.BlockSpec((1, n_heads, tc), lambda bi, c: (bi, 0, c)),
        out_shape=jax.ShapeDtypeStruct((b, n_heads, t), F32),
        scratch_shapes=[pltpu.VMEM((1, LANES), F32)],
        compiler_params=_cp(2),
        name="logf_cumsum",
    )(logf_pad)


def _gla_levels():
    return [GLA_CHUNK >> (i + 1) for i in range(int(math.log2(GLA_CHUNK)))]


def _gla_masks():
    t = np.arange(GLA_CHUNK)[:, None]
    s = np.arange(GLA_CHUNK)[None, :]
    masks = []
    for b in _gla_levels():
        masks.append((t // (2 * b) == s // (2 * b)) & (t % (2 * b) >= b) & (s % (2 * b) < b))
    masks.append(t == s)
    return jnp.asarray(np.stack(masks).astype(np.float32))


def _level_reference(g_ref, b):
    row = lambda r, n: jnp.broadcast_to(g_ref[r:r + 1, :], (n, LANES))
    if 2 * b >= SUBLANES:
        return jnp.concatenate([row(2 * b * j + b - 1, 2 * b) for j in range(GLA_CHUNK // (2 * b))], axis=0)
    sub = lax.broadcasted_iota(jnp.int32, (SUBLANES, LANES), 0)
    pieces = []
    for j in range(GLA_CHUNK // SUBLANES):
        base = SUBLANES * j
        n_blk = SUBLANES // (2 * b)
        piece = row(base + (n_blk - 1) * 2 * b + b - 1, SUBLANES)
        for i in range(n_blk - 2, -1, -1):
            piece = jnp.where(sub < (i + 1) * 2 * b, row(base + i * 2 * b + b - 1, SUBLANES), piece)
        pieces.append(piece)
    return jnp.concatenate(pieces, axis=0)


def _lower_bound(lb_ref, layer_idx):
    x = lb_ref[...].astype(F32)
    e = jnp.exp(x - jnp.max(x, axis=0, keepdims=True))
    p = e / jnp.sum(e, axis=0, keepdims=True)
    r = lax.broadcasted_iota(jnp.int32, p.shape, 0)
    return jnp.sum(jnp.where((r >= 1) & (r <= layer_idx), p, 0.0), axis=0, keepdims=True)


def _gla_kernel(q_ref, f_ref, i_ref, g_ref, lb_ref, gn_ref, mask_ref, o_ref, s_ref,
                st_ref, gs_ref, *, layer_idx, n_chunks):
    c = pl.program_id(2)
    scale = HEAD ** -0.5

    @pl.when(c == 0)
    def _():
        st_ref[...] = jnp.zeros(st_ref.shape, F32)

    lb = _lower_bound(lb_ref, layer_idx)
    q = _silu(q_ref[...]) * scale
    gate = lb + (1.0 - lb) * _sigmoid(f_ref[...])
    kk = 1.0 - gate
    v = i_ref[...]
    v_bf = v.astype(BF16)
    g_cum = _cumsum_rows(jnp.log(gate), GLA_CHUNK)
    gs_ref[...] = g_cum
    st = st_ref[...]
    o = _nt_dot((q * jnp.exp(g_cum)).astype(BF16), st.astype(BF16))

    levels = _gla_levels()
    a = jnp.sum(q * kk, axis=1, keepdims=True) * mask_ref[len(levels)]
    for lvl, b in enumerate(levels):
        d = g_cum - _level_reference(gs_ref, b)
        qs = (q * jnp.exp(jnp.minimum(d, 0.0))).astype(BF16)
        ks = (kk * jnp.exp(jnp.minimum(-d, 0.0))).astype(BF16)
        a = a + _nt_dot(qs, ks) * mask_ref[lvl]
    o = o + jnp.dot(a.astype(BF16), v_bf, preferred_element_type=F32)

    g_last = gs_ref[GLA_CHUNK - 1:GLA_CHUNK, :]
    k_dec = (kk * jnp.exp(g_last - g_cum)).astype(BF16)
    st_new = st * jnp.exp(g_last) + jnp.dot(v.T.astype(BF16), k_dec, preferred_element_type=F32)
    st_ref[...] = st_new

    ms = jnp.mean(o * o, axis=-1, keepdims=True)
    y = o * lax.rsqrt(ms + EPS) * gn_ref[...]
    o_ref[...] = (y * _silu(g_ref[...])).astype(o_ref.dtype)

    @pl.when(c == n_chunks - 1)
    def _():
        s_ref[...] = st_new.T


def _gla_prompt(proj, lower_bounds, g_norm, layer_idx, b, t):
    m, d4 = proj.shape
    d = d4 // 4
    n_heads = d // HEAD
    depth = lower_bounds.shape[0]
    n_chunks = t // GLA_CHUNK
    masks = _gla_masks()
    blk = lambda off: pl.BlockSpec((GLA_CHUNK, HEAD), lambda bi, h, c: (bi * n_chunks + c, off * n_heads + h))
    return pl.pallas_call(
        functools.partial(_gla_kernel, layer_idx=layer_idx, n_chunks=n_chunks),
        grid=(b, n_heads, n_chunks),
        in_specs=[blk(0), blk(1), blk(2), blk(3),
                  pl.BlockSpec((depth, HEAD), lambda bi, h, c: (0, h)),
                  pl.BlockSpec((1, HEAD), lambda bi, h, c: (0, 0)),
                  pl.BlockSpec(masks.shape, lambda bi, h, c: (0, 0, 0))],
        out_specs=[pl.BlockSpec((GLA_CHUNK, HEAD), lambda bi, h, c: (bi * n_chunks + c, h)),
                   pl.BlockSpec((None, None, HEAD, HEAD), lambda bi, h, c: (bi, h, 0, 0))],
        out_shape=[jax.ShapeDtypeStruct((m, d), BF16),
                   jax.ShapeDtypeStruct((b, n_heads, HEAD, HEAD), F32)],
        scratch_shapes=[pltpu.VMEM((HEAD, HEAD), F32), pltpu.VMEM((GLA_CHUNK, HEAD), F32)],
        compiler_params=_cp(3),
        name="gla_prompt",
    )(proj, proj, proj, proj, lower_bounds, g_norm.reshape(1, HEAD).astype(F32), masks)


def _gla_step_kernel(i_ref, g_ref, qt_ref, ft_ref, lbt_ref, gn_ref, s_in_ref, o_ref, s_out_ref,
                     *, layer_idx, n_batch):
    scale = HEAD ** -0.5
    xt = lbt_ref[...].astype(F32)
    et = jnp.exp(xt - jnp.max(xt, axis=1, keepdims=True))
    pt = et / jnp.sum(et, axis=1, keepdims=True)
    ct = lax.broadcasted_iota(jnp.int32, pt.shape, 1)
    lb_col = jnp.sum(jnp.where((ct >= 1) & (ct <= layer_idx), pt, 0.0), axis=1, keepdims=True)

    q_col = _silu(qt_ref[...]) * scale
    gate_col = lb_col + (1.0 - lb_col) * _sigmoid(ft_ref[...])
    k_col = 1.0 - gate_col
    a_all = jnp.sum(q_col * k_col, axis=0, keepdims=True)
    qg_col = q_col * gate_col
    for bi in range(n_batch):
        s_old = s_in_ref[bi]
        v_row = i_ref[bi:bi + 1, :]
        o = (jnp.sum(qg_col[:, bi:bi + 1] * s_old, axis=0, keepdims=True)
             + a_all[:, bi:bi + 1] * v_row)
        s_out_ref[bi] = gate_col[:, bi:bi + 1] * s_old + k_col[:, bi:bi + 1] * v_row
        ms = jnp.mean(o * o, axis=-1, keepdims=True)
        y = o * lax.rsqrt(ms + EPS) * gn_ref[...]
        o_ref[bi:bi + 1, :] = y * _silu(g_ref[bi:bi + 1, :])


def _gla_step(proj, state, layer_j, lower_bounds, g_norm, layer_idx):
    nb, d4 = proj.shape
    d = d4 // 4
    n_heads = d // HEAD
    depth = lower_bounds.shape[0]
    proj_t = proj.T
    lb_t = lower_bounds.T
    row = lambda off: pl.BlockSpec((nb, HEAD), lambda h: (0, off * n_heads + h))
    col = lambda off: pl.BlockSpec((HEAD, nb), lambda h: (off * n_heads + h, 0))
    return pl.pallas_call(
        functools.partial(_gla_step_kernel, layer_idx=layer_idx, n_batch=nb),
        grid=(n_heads,),
        in_specs=[row(2), row(3), col(0), col(1),
                  pl.BlockSpec((HEAD, depth), lambda h: (h, 0)),
                  pl.BlockSpec((1, HEAD), lambda h: (0, 0)),
                  pl.BlockSpec((None, nb, None, HEAD, HEAD), lambda h: (layer_j, 0, h, 0, 0))],
        out_specs=[pl.BlockSpec((nb, HEAD), lambda h: (0, h)),
                   pl.BlockSpec((nb, None, HEAD, HEAD), lambda h: (0, h, 0, 0))],
        out_shape=[jax.ShapeDtypeStruct((nb, d), F32),
                   jax.ShapeDtypeStruct((nb, n_heads, HEAD, HEAD), F32)],
        compiler_params=_cp(1),
        name="gla_step",
    )(proj, proj, proj_t, proj_t, lb_t, g_norm.reshape(1, HEAD).astype(F32), state)


def _decode_kernel(pt_ref, *refs, kind, n_pages, page, n_rows, per_step, lam_init):
    it = iter(refs)
    q_ref = next(it)
    kn_ref = next(it)
    vn_ref = next(it)
    kc_refs = [next(it) for _ in range(per_step)]
    vc_refs = [next(it) for _ in range(per_step)]
    if kind == "c":
        lfc_refs = [next(it) for _ in range(per_step)]
        lfn_ref = next(it)
    else:
        lq1, lk1, lq2, lk2 = next(it), next(it), next(it), next(it)
    o_ref = next(it)
    qbf_ref, mask_ref, m_ref, l_ref, acc_ref = next(it), next(it), next(it), next(it), next(it)
    if kind == "c":
        carry_ref = next(it)
    else:
        pair_ref = next(it)

    p = pl.program_id(1)
    scale = HEAD ** -0.5
    group = n_rows if kind == "c" else n_rows // 2
    ncols = page * group
    rows_col = lax.broadcasted_iota(jnp.int32, (n_rows, 1), 0)

    @pl.when(p == 0)
    def _():
        qbf_ref[...] = q_ref[0].astype(BF16)
        r = lax.broadcasted_iota(jnp.int32, (n_rows, ncols), 0)
        cidx = lax.broadcasted_iota(jnp.int32, (n_rows, ncols), 1)
        head_of_row = r if kind == "c" else r // 2
        mask_ref[...] = jnp.where(cidx % group == head_of_row, 0.0, NEG_INF)
        m_ref[...] = jnp.full(m_ref.shape, NEG_INF, F32)
        l_ref[...] = jnp.zeros(l_ref.shape, F32)
        acc_ref[...] = jnp.zeros(acc_ref.shape, F32)
        if kind == "c":
            carry_ref[...] = lfn_ref[0]

    lane = lax.broadcasted_iota(jnp.int32, (1, ncols), 1)
    qbf = qbf_ref[...]
    mask = mask_ref[...]
    if kind == "a":
        slope = jnp.exp2(-8.0 * ((rows_col // 2).astype(F32) + 1.0) / group)
    else:
        carry = carry_ref[...]
    scores, values = [], []
    for g in range(per_step):
        page_idx = n_pages - 1 - (p * per_step + g)
        kc_ref, vc_ref = kc_refs[g], vc_refs[g]
        if kind == "a":
            s_even = _nt_dot(qbf, kc_ref[pl.ds(0, ncols, stride=2), :].astype(BF16))
            s_odd = _nt_dot(qbf, kc_ref[pl.ds(1, ncols, stride=2), :].astype(BF16))
            s = jnp.where(rows_col % 2 == 0, s_even, s_odd) * scale
            dist = (n_pages * page - (page_idx * page + lane // group)).astype(F32)
            s = s - slope * dist + mask
            values.append(vc_ref[...].reshape(ncols, vc_ref.shape[-1]))
        else:
            s = _nt_dot(qbf, kc_ref[...].astype(BF16)) * scale
            lf2d = lfc_refs[g][...]
            flat = jnp.concatenate([lf2d[i:i + 1, :] for i in range(lf2d.shape[0])], axis=1)
            y = flat
            tot = flat
            dd = group
            while dd < ncols:
                y = y + jnp.where(lane + dd < ncols, pltpu.roll(y, ncols - dd, 1), 0.0)
                tot = tot + pltpu.roll(tot, dd, 1)
                dd *= 2
            s = s + (y - flat + carry) + mask
            carry = carry + tot
            values.append(vc_ref[...])
        scores.append(s)
    if kind == "c":
        carry_ref[...] = carry

    m_prev = m_ref[...]
    m_new = m_prev
    for s in scores:
        m_new = jnp.maximum(m_new, jnp.max(s, axis=1, keepdims=True))
    alpha = jnp.exp(m_prev - m_new)
    l_new = alpha * l_ref[...]
    acc = alpha * acc_ref[...]
    for s, v2d in zip(scores, values):
        pr = jnp.exp(s - m_new)
        l_new = l_new + jnp.sum(pr, axis=1, keepdims=True)
        acc = acc + jnp.dot(pr.astype(BF16), v2d.astype(BF16), preferred_element_type=F32)
    l_ref[...] = l_new
    acc_ref[...] = acc
    m_ref[...] = m_new

    @pl.when(p == n_pages // per_step - 1)
    def _():
        s_new = jnp.sum(q_ref[0] * kn_ref[0], axis=1, keepdims=True) * scale
        m_prev = m_ref[...]
        m_fin = jnp.maximum(m_prev, s_new)
        alpha = jnp.exp(m_prev - m_fin)
        p_new = jnp.exp(s_new - m_fin)
        l_fin = alpha * l_ref[...] + p_new
        acc = alpha * acc_ref[...] + p_new * vn_ref[0]
        o_full = acc * (1.0 / l_fin)
        if kind == "a":
            lam = _lam_value(lq1, lk1, lq2, lk2, lam_init)
            signed = o_full * jnp.where(rows_col % 2 == 0, 1.0, -lam)
            for c in range(pair_ref.shape[0]):
                pair_ref[c] = signed[:, c * LANES:(c + 1) * LANES]
                o_ref[0, :, c * LANES:(c + 1) * LANES] = (pair_ref[c, pl.ds(0, group, stride=2), :]
                                                          + pair_ref[c, pl.ds(1, group, stride=2), :])
        else:
            o_ref[0] = o_full


def _decode_attention(kind, q, k_new, v_new, cache_k, cache_v, layer_j, page_table, extras, lam_init=0.0):
    nb, n_rows, _ = q.shape
    vw = v_new.shape[-1]
    n_pages = page_table.shape[1]
    page = cache_k.shape[2] // n_rows
    group = n_rows if kind == "c" else n_rows // 2
    ncols = page * group
    per_step = math.gcd(DECODE_PAGES, n_pages)
    per_b = lambda shape: pl.BlockSpec((1,) + shape, lambda b, p, pt: (b,) + (0,) * len(shape))

    def paged(shape, g):
        return pl.BlockSpec((None, None) + shape,
                            lambda b, p, pt: (layer_j, pt[b, n_pages - 1 - (p * per_step + g)]) + (0,) * len(shape))

    steps = range(per_step)
    in_specs = ([per_b((n_rows, HEAD)), per_b((n_rows, HEAD)), per_b((n_rows, vw))]
                + [paged((page * n_rows, HEAD), g) for g in steps]
                + [paged(cache_v.shape[2:], g) for g in steps])
    args = [q, k_new, v_new] + [cache_k] * per_step + [cache_v] * per_step
    scratch = [pltpu.VMEM((n_rows, HEAD), BF16), pltpu.VMEM((n_rows, ncols), F32),
               pltpu.VMEM((n_rows, 1), F32), pltpu.VMEM((n_rows, 1), F32), pltpu.VMEM((n_rows, vw), F32)]
    if kind == "c":
        cache_logf, logf_new = extras
        in_specs += [paged(cache_logf.shape[2:], g) for g in steps] + [per_b((1, ncols))]
        args += [cache_logf] * per_step + [logf_new]
        scratch += [pltpu.VMEM((1, ncols), F32)]
    else:
        in_specs += [pl.BlockSpec((1, HEAD), lambda b, p, pt: (0, 0))] * 4
        args += [e.reshape(1, HEAD).astype(F32) for e in extras]
        scratch += [pltpu.VMEM((vw // LANES, n_rows, LANES), F32)]
    return pl.pallas_call(
        functools.partial(_decode_kernel, kind=kind, n_pages=n_pages, page=page, n_rows=n_rows,
                          per_step=per_step, lam_init=lam_init),
        grid_spec=pltpu.PrefetchScalarGridSpec(
            num_scalar_prefetch=1,
            grid=(nb, n_pages // per_step),
            in_specs=in_specs,
            out_specs=pl.BlockSpec((1, group, vw), lambda b, p, pt: (b, 0, 0)),
            scratch_shapes=scratch),
        out_shape=jax.ShapeDtypeStruct((nb, group, vw), F32),
        compiler_params=_cp(2),
        name="decode_" + kind,
    )(page_table, *args)


def _layer_a(hp, hs, xp, xs, cache_k, cache_v, j, page_table, w_in, q_norm, k_norm, lams, subln_g, w_out,
             layer_idx, b, t):
    d = xp.shape[1]
    lam_init = 0.8 - 0.6 * math.exp(-0.3 * layer_idx)
    n_layers, n_phys, page, n_rows, _ = cache_k.shape
    ck = cache_k.reshape(n_layers, n_phys, page * n_rows, HEAD)

    q = _matmul(hp, w_in, layer=j, col0=0, ncols=d, mode="gnorm", aux=q_norm, out_dtypes=(BF16,))
    k32, k16 = _matmul(hp, w_in, layer=j, col0=d, ncols=d, mode="gnorm", aux=k_norm, out_dtypes=(F32, BF16))
    v32, v16 = _matmul(hp, w_in, layer=j, col0=2 * d, ncols=d, out_dtypes=(F32, BF16))
    o = _attn_a_prompt(q.reshape(b, t, d), k16.reshape(b, t, d), v16.reshape(b, t, d), *lams, subln_g, lam_init)
    xp = _matmul(o.reshape(b * t, d), w_out, layer=j, mode="res", aux=xp)

    nb = hs.shape[0]
    qs = _matmul(hs, w_in, layer=j, col0=0, ncols=d, mode="gnorm", aux=q_norm)
    ks = _matmul(hs, w_in, layer=j, col0=d, ncols=d, mode="gnorm", aux=k_norm)
    vs = _matmul(hs, w_in, layer=j, col0=2 * d, ncols=d)
    v_rows = jnp.repeat(vs.reshape(nb, n_rows // 2, 2 * HEAD), 2, axis=1)
    os_ = _decode_attention("a", qs.reshape(nb, n_rows, HEAD), ks.reshape(nb, n_rows, HEAD), v_rows,
                            ck, cache_v, j, page_table, lams, lam_init)
    os_ = _rmsnorm(os_.reshape(nb * n_rows // 2, 2 * HEAD), subln_g, F32, post_scale=1.0 - lam_init)
    xs = _matmul(os_.reshape(nb, d), w_out, layer=j, mode="res", aux=xs)
    return xp, xs, k32, v32, ks, vs


def _layer_b(hp, hs, xp, xs, state, j, w_in, lower_bounds, g_norm, w_out, layer_idx, b, t):
    proj = _matmul(hp, w_in, layer=j)
    o, s_p = _gla_prompt(proj, lower_bounds, g_norm, layer_idx, b, t)
    xp = _matmul(o, w_out, layer=j, mode="res", aux=xp)

    proj_s = _matmul(hs, w_in, layer=j)
    o_s, s_s = _gla_step(proj_s, state, j, lower_bounds, g_norm, layer_idx)
    xs = _matmul(o_s, w_out, layer=j, mode="res", aux=xs)
    return xp, xs, s_p, s_s


def _layer_c(hp, hs, xp, xs, cache_k, cache_v, cache_logf, j, page_table, w_in, b_f, q_norm, k_norm, w_out, b, t):
    d = xp.shape[1]
    n_heads = d // HEAD
    n_layers, n_phys, page = cache_k.shape[:3]
    ck = cache_k.reshape(n_layers, n_phys, page * n_heads, HEAD)
    cv = cache_v.reshape(n_layers, n_phys, page * n_heads, HEAD)
    clf = cache_logf.reshape(n_layers, n_phys, page * n_heads // LANES, LANES)
    w_f = jnp.pad(w_in[j, :, 3 * d:], ((0, 0), (0, LANES - n_heads)))
    b_pad = jnp.pad(b_f.astype(F32), (0, LANES - n_heads))

    q = _matmul(hp, w_in, layer=j, col0=0, ncols=d, mode="gnorm", aux=q_norm, out_dtypes=(BF16,))
    k32, k16 = _matmul(hp, w_in, layer=j, col0=d, ncols=d, mode="gnorm", aux=k_norm, out_dtypes=(F32, BF16))
    v32, v16 = _matmul(hp, w_in, layer=j, col0=2 * d, ncols=d, out_dtypes=(F32, BF16))
    logf = _matmul(hp, w_f, mode="logsig", aux=b_pad)
    f_t = _logf_cumsum_t(logf.reshape(b, t, LANES), n_heads)
    o = _attn_c_prompt(q.reshape(b, t, d), k16.reshape(b, t, d), v16.reshape(b, t, d),
                       f_t.reshape(b, n_heads, 1, t))
    xp = _matmul(o.reshape(b * t, d), w_out, layer=j, mode="res", aux=xp)

    nb = hs.shape[0]
    qs = _matmul(hs, w_in, layer=j, col0=0, ncols=d, mode="gnorm", aux=q_norm)
    ks = _matmul(hs, w_in, layer=j, col0=d, ncols=d, mode="gnorm", aux=k_norm)
    vs = _matmul(hs, w_in, layer=j, col0=2 * d, ncols=d)
    lfs = _matmul(hs, w_f, mode="logsig", aux=b_pad)[:, :n_heads]
    lf_new = jnp.tile(lfs, (1, page)).reshape(nb, 1, page * n_heads)
    os_ = _decode_attention("c", qs.reshape(nb, n_heads, HEAD), ks.reshape(nb, n_heads, HEAD),
                            vs.reshape(nb, n_heads, HEAD), ck, cv, j, page_table, (clf, lf_new))
    xs = _matmul(os_.reshape(nb, d), w_out, layer=j, mode="res", aux=xs)
    return xp, xs, k32, v32, logf[:, :n_heads], ks, vs, lfs


def _ffn(x, g, w_gu, w_down, layer, h_dtype):
    d_ff = w_down.shape[1]
    h = _rmsnorm(x, g, h_dtype)
    act = _matmul(h, w_gu, layer=layer, col0=0, ncols=d_ff, col1=d_ff, mode="swiglu", out_dtypes=(BF16,),
                  tn=512)
    return _matmul(act, w_down, layer=layer, mode="res", aux=x, tn=512)


def kernel(x_prompt, x_sample, cache_k_a, cache_v_a, state_s_b, cache_k_c, cache_v_c, cache_logf_c, page_table, norm1_g, norm2_g, a_w_in, a_q_norm, a_k_norm, a_lam_q1, a_lam_k1, a_lam_q2, a_lam_k2, a_subln_g, a_w_out, b_w_in, b_lower_bounds, b_out_norm_g, b_w_out, c_w_in, c_b_f, c_q_norm, c_k_norm, c_w_out, ffn_w_gu, ffn_w_down):
    b, t, d = x_prompt.shape
    nb, ds, _ = x_sample.shape
    assert ds == 1 and d % (2 * HEAD) == 0 and t % GLA_CHUNK == 0
    depth = norm1_g.shape[0]
    xp = x_prompt.reshape(b * t, d)
    xs = x_sample.reshape(nb, d)
    h_a, h_c = d // (2 * HEAD), d // HEAD
    outs = {name: [] for name in ("ka_p", "va_p", "ka_s", "va_s", "sb_p", "sb_s",
                                  "kc_p", "vc_p", "lc_p", "kc_s", "vc_s", "lc_s")}
    for i in range(depth):
        j = i // 3
        hp = _rmsnorm(xp, norm1_g[i], BF16)
        hs = _rmsnorm(xs, norm1_g[i], F32)
        kind = i % 3
        if kind == 0:
            lams = (a_lam_q1[j], a_lam_k1[j], a_lam_q2[j], a_lam_k2[j])
            xp, xs, k1, v1, k2, v2 = _layer_a(hp, hs, xp, xs, cache_k_a, cache_v_a, j, page_table, a_w_in,
                                              a_q_norm[j], a_k_norm[j], lams, a_subln_g[j], a_w_out, i, b, t)
            outs["ka_p"].append(k1.reshape(b, t, 2 * h_a, HEAD))
            outs["va_p"].append(v1.reshape(b, t, h_a, 2 * HEAD))
            outs["ka_s"].append(k2.reshape(nb, 1, 2 * h_a, HEAD))
            outs["va_s"].append(v2.reshape(nb, 1, h_a, 2 * HEAD))
        elif kind == 1:
            xp, xs, s1, s2 = _layer_b(hp, hs, xp, xs, state_s_b, j, b_w_in, b_lower_bounds,
                                      b_out_norm_g[j], b_w_out, i, b, t)
            outs["sb_p"].append(s1)
            outs["sb_s"].append(s2)
        else:
            xp, xs, k1, v1, l1, k2, v2, l2 = _layer_c(hp, hs, xp, xs, cache_k_c, cache_v_c, cache_logf_c, j,
                                                      page_table, c_w_in, c_b_f[j], c_q_norm[j],
                                                      c_k_norm[j], c_w_out, b, t)
            outs["kc_p"].append(k1.reshape(b, t, h_c, HEAD))
            outs["vc_p"].append(v1.reshape(b, t, h_c, HEAD))
            outs["lc_p"].append(l1.reshape(b, t, h_c))
            outs["kc_s"].append(k2.reshape(nb, 1, h_c, HEAD))
            outs["vc_s"].append(v2.reshape(nb, 1, h_c, HEAD))
            outs["lc_s"].append(l2.reshape(nb, 1, h_c))
        xp = _ffn(xp, norm2_g[i], ffn_w_gu, ffn_w_down, i, BF16)
        xs = _ffn(xs, norm2_g[i], ffn_w_gu, ffn_w_down, i, F32)
    st = lambda name: jnp.stack(outs[name])
    return (xp.reshape(b, t, d), xs.reshape(nb, 1, d), st("ka_p"), st("va_p"), st("ka_s"), st("va_s"),
            st("sb_p"), st("sb_s"), st("kc_p"), st("vc_p"), st("lc_p"), st("kc_s"), st("vc_s"), st("lc_s"))
```

```python
import functools
import math

import numpy as np
import jax
import jax.numpy as jnp
from jax import lax
from jax.experimental import pallas as pl
from jax.experimental.pallas import tpu as pltpu

F32 = jnp.float32
BF16 = jnp.bfloat16
EPS = 1e-6
LANES = 128
SUBLANES = 8
VMEM_LIMIT = 56 * 1024 * 1024
HEAD = 128
GLA_CHUNK = 128
ATT_TQ = 256
ATT_CK = 512
DECODE_PAGES = 8
GLA_HEADS = 4
LOG2E = 1.4426950408889634
NEG_INF = float("-inf")


def _cp(n_axes):
    return pltpu.CompilerParams(dimension_semantics=("arbitrary",) * n_axes,
                                vmem_limit_bytes=VMEM_LIMIT)


def _sigmoid(x):
    return 1.0 / (1.0 + jnp.exp(-x))


def _silu(x):
    return x * _sigmoid(x)


def _log_sigmoid(x):
    return jnp.minimum(x, 0.0) - jnp.log1p(jnp.exp(-jnp.abs(x)))


def _nt_dot(a, b):
    return lax.dot_general(a, b, (((1,), (1,)), ((), ())), preferred_element_type=F32)


def _rmsnorm_kernel(x_ref, g_ref, o_ref, *, post_scale):
    x = x_ref[...].astype(F32)
    ms = jnp.mean(x * x, axis=-1, keepdims=True)
    y = x * lax.rsqrt(ms + EPS) * g_ref[...]
    if post_scale != 1.0:
        y = y * post_scale
    o_ref[...] = y.astype(o_ref.dtype)


def _rmsnorm(x, g, out_dtype, post_scale=1.0):
    rows, d = x.shape
    tr = min(rows, 512)
    return pl.pallas_call(
        functools.partial(_rmsnorm_kernel, post_scale=post_scale),
        grid=(rows // tr,),
        in_specs=[pl.BlockSpec((tr, d), lambda i: (i, 0)),
                  pl.BlockSpec((1, d), lambda i: (0, 0))],
        out_specs=pl.BlockSpec((tr, d), lambda i: (i, 0)),
        out_shape=jax.ShapeDtypeStruct((rows, d), out_dtype),
        compiler_params=_cp(1),
        name="rmsnorm",
    )(x, g.reshape(1, d).astype(F32))


def _mm_epilogue(acc, up, aux, out_refs, mode, tn):
    if mode == "swiglu":
        acc = _silu(acc) * up
    elif mode == "res":
        acc = aux + acc
    elif mode == "logsig":
        acc = _log_sigmoid(acc + aux)
    if mode == "gnorm":
        for c in range(tn // HEAD):
            blk = acc[:, c * HEAD:(c + 1) * HEAD]
            ms = jnp.mean(blk * blk, axis=-1, keepdims=True)
            y = blk * lax.rsqrt(ms + EPS) * aux
            for o_ref in out_refs:
                o_ref[:, c * HEAD:(c + 1) * HEAD] = y.astype(o_ref.dtype)
    else:
        for o_ref in out_refs:
            o_ref[...] = acc.astype(o_ref.dtype)


def _mm_kernel(*refs, mode, n_out, n_out_s, tn):
    dual = mode == "swiglu"
    has_aux = mode in ("gnorm", "res", "logsig")
    it = iter(refs)
    x_ref = next(it)
    w_ref = next(it)
    w2_ref = next(it) if dual else None
    aux_ref = next(it) if has_aux else None
    xs_ref = next(it)
    aux_s_ref = next(it) if mode == "res" else aux_ref
    out_refs = [next(it) for _ in range(n_out)]
    out_s_refs = [next(it) for _ in range(n_out_s)]
    wbf_ref = next(it)
    wbf2_ref = next(it) if dual else None

    @pl.when(pl.program_id(1) == 0)
    def _():
        wbf_ref[...] = w_ref[...].astype(BF16)
        if dual:
            wbf2_ref[...] = w2_ref[...].astype(BF16)
        xs = xs_ref[...].astype(BF16)
        acc_s = jnp.dot(xs, wbf_ref[...], preferred_element_type=F32)
        up_s = jnp.dot(xs, wbf2_ref[...], preferred_element_type=F32) if dual else None
        _mm_epilogue(acc_s, up_s, aux_s_ref[...] if has_aux else None, out_s_refs, mode, tn)

    x = x_ref[...].astype(BF16)
    acc = jnp.dot(x, wbf_ref[...], preferred_element_type=F32)
    up = jnp.dot(x, wbf2_ref[...], preferred_element_type=F32) if dual else None
    _mm_epilogue(acc, up, aux_ref[...] if has_aux else None, out_refs, mode, tn)


def _matmul(x, xs, w, *, layer=0, col0=0, ncols=None, mode="plain", out_dtypes=(F32,), out_dtypes_s=(F32,),
            aux=None, aux_s=None, col1=None, tm=512, tn=1024):
    if w.ndim == 2:
        w = w[None]
    m, k = x.shape
    ms = xs.shape[0]
    ncols = w.shape[2] - col0 if ncols is None else ncols
    tm = min(tm, m)
    tn = min(tn, ncols)
    assert m % tm == 0 and ncols % tn == 0 and col0 % tn == 0
    nb0 = col0 // tn
    in_specs = [pl.BlockSpec((tm, k), lambda n, i: (i, 0)),
                pl.BlockSpec((None, k, tn), lambda n, i: (layer, 0, n + nb0))]
    args = [x, w]
    scratch = [pltpu.VMEM((k, tn), BF16)]
    if mode == "swiglu":
        assert col1 % tn == 0
        nb1 = col1 // tn
        in_specs.append(pl.BlockSpec((None, k, tn), lambda n, i: (layer, 0, n + nb1)))
        args.append(w)
        scratch.append(pltpu.VMEM((k, tn), BF16))
    if mode == "gnorm":
        in_specs.append(pl.BlockSpec((1, HEAD), lambda n, i: (0, 0)))
        args.append(aux.reshape(1, HEAD).astype(F32))
    elif mode == "res":
        in_specs.append(pl.BlockSpec((tm, tn), lambda n, i: (i, n)))
        args.append(aux)
    elif mode == "logsig":
        in_specs.append(pl.BlockSpec((1, tn), lambda n, i: (0, n)))
        args.append(aux.reshape(1, ncols).astype(F32))
    in_specs.append(pl.BlockSpec((ms, k), lambda n, i: (0, 0)))
    args.append(xs)
    if mode == "res":
        in_specs.append(pl.BlockSpec((ms, tn), lambda n, i: (0, n)))
        args.append(aux_s)
    outs = pl.pallas_call(
        functools.partial(_mm_kernel, mode=mode, n_out=len(out_dtypes), n_out_s=len(out_dtypes_s), tn=tn),
        grid=(ncols // tn, m // tm),
        in_specs=in_specs,
        out_specs=([pl.BlockSpec((tm, tn), lambda n, i: (i, n)) for _ in out_dtypes]
                   + [pl.BlockSpec((ms, tn), lambda n, i: (0, n)) for _ in out_dtypes_s]),
        out_shape=([jax.ShapeDtypeStruct((m, ncols), dt) for dt in out_dtypes]
                   + [jax.ShapeDtypeStruct((ms, ncols), dt) for dt in out_dtypes_s]),
        scratch_shapes=scratch,
        compiler_params=_cp(2),
        name="mm_" + mode,
    )(*args)
    return tuple(outs[:len(out_dtypes)]), tuple(outs[len(out_dtypes):])


def _lam_value(lq1, lk1, lq2, lk2, lam_init):
    a = jnp.sum(lq1[...] * lk1[...], axis=-1, keepdims=True)
    b = jnp.sum(lq2[...] * lk2[...], axis=-1, keepdims=True)
    return jnp.exp(a) - jnp.exp(b) + lam_init


def _causal_rows(qs, k_ref, v_ref, s_ref, bias_of_chunk, qi, n_chunks, tq, ck):
    cs = (HEAD ** -0.5) * LOG2E
    n_maps = len(qs)
    row = lax.broadcasted_iota(jnp.int32, (tq, ck), 0)
    col = lax.broadcasted_iota(jnp.int32, (tq, ck), 1)
    m_part = [jnp.full((tq, LANES), NEG_INF, F32) for _ in range(n_maps)]
    for c in range(n_chunks):
        k = k_ref[0, c * ck:(c + 1) * ck, :]
        bias = bias_of_chunk(c)
        for j in range(n_maps):
            s = _nt_dot(qs[j], k[:, j * HEAD:(j + 1) * HEAD]) * cs + bias
            if c == n_chunks - 1:
                s = jnp.where(row - col >= c * ck - qi * tq, s, NEG_INF)
            s_ref[j, :, c * ck:(c + 1) * ck] = s
            for g in range(ck // LANES):
                m_part[j] = jnp.maximum(m_part[j], s[:, g * LANES:(g + 1) * LANES])
    def chunk_dot(w, c):
        return jnp.dot(w.astype(BF16), v_ref[0, c * ck:(c + 1) * ck, :], preferred_element_type=F32)

    outs = []
    for j in range(n_maps):
        m = jnp.max(m_part[j], axis=1, keepdims=True)
        l_part = jnp.zeros((tq, LANES), F32)
        o = None
        for c in range(n_chunks):
            p = jnp.exp2(s_ref[j, :, c * ck:(c + 1) * ck] - m)
            for g in range(ck // LANES):
                l_part = l_part + p[:, g * LANES:(g + 1) * LANES]
            o = chunk_dot(p, c) if o is None else o + chunk_dot(p, c)
        outs.append(o * (1.0 / jnp.sum(l_part, axis=1, keepdims=True)))
    return outs


def _for_each_chunk_count(qi, tq, ck, t, fn):
    per = ck // tq
    for n_chunks in range(1, t // ck + 1):
        pl.when(qi // per == n_chunks - 1)(functools.partial(fn, n_chunks))


def _attn_a_kernel(q_ref, k_ref, v_ref, lq1, lk1, lq2, lk2, sg_ref, o_ref, s_ref,
                   *, tq, ck, t, n_heads, lam_init):
    h = pl.program_id(1)
    qi = pl.program_id(2)
    slope2 = LOG2E * jnp.exp2(-8.0 * (jnp.full((1, 1), h, jnp.int32).astype(F32) + 1.0) / n_heads)
    lam = _lam_value(lq1, lk1, lq2, lk2, lam_init)
    col = lax.broadcasted_iota(jnp.int32, (1, ck), 1)
    q = q_ref[0]
    qs = [q[:, j * HEAD:(j + 1) * HEAD] for j in range(2)]

    def run(n_chunks):
        bias = lambda c: slope2 * (col + (c * ck - qi * tq)).astype(F32)
        o1, o2 = _causal_rows(qs, k_ref, v_ref, s_ref, bias, qi, n_chunks, tq, ck)
        o = o1 - lam * o2
        ms =jnp.mean(o * o, axis=-1, keepdims=True)
        y = o * lax.rsqrt(ms + EPS) * sg_ref[...]
        o_ref[0] = (y * (1.0 - lam_init)).astype(o_ref.dtype)

    _for_each_chunk_count(qi, tq, ck, t, run)


def _attn_tiles(t):
    tq = min(ATT_TQ, t)
    ck = min(ATT_CK, t)
    assert t % ck == 0 and ck % tq == 0
    return tq, ck


def _attn_a_prompt(q, k, v, lq1, lk1, lq2, lk2, subln_g, lam_init):
    b, t, d = q.shape
    n_heads = d // (2 * HEAD)
    tq, ck = _attn_tiles(t)
    vec = lambda a: a.reshape(1, -1).astype(F32)
    small = lambda w: pl.BlockSpec((1, w), lambda bi, h, qi: (0, 0))
    return pl.pallas_call(
        functools.partial(_attn_a_kernel, tq=tq, ck=ck, t=t, n_heads=n_heads, lam_init=lam_init),
        grid=(b, n_heads, t // tq),
        in_specs=[pl.BlockSpec((1, tq, 2 * HEAD), lambda bi, h, qi: (bi, qi, h)),
                  pl.BlockSpec((1, t, 2 * HEAD), lambda bi, h, qi: (bi, 0, h)),
                  pl.BlockSpec((1, t, 2 * HEAD), lambda bi, h, qi: (bi, 0, h)),
                  small(HEAD), small(HEAD), small(HEAD), small(HEAD), small(2 * HEAD)],
        out_specs=pl.BlockSpec((1, tq, 2 * HEAD), lambda bi, h, qi: (bi, qi, h)),
        out_shape=jax.ShapeDtypeStruct((b, t, d), BF16),
        scratch_shapes=[pltpu.VMEM((2, tq, t), F32)],
        compiler_params=_cp(3),
        name="attn_a_prompt",
    )(q, k, v, vec(lq1), vec(lk1), vec(lq2), vec(lk2), vec(subln_g))


def _attn_c_kernel(q_ref, k_ref, v_ref, f_ref, o_ref, s_ref, *, tq, ck, t):
    qi = pl.program_id(2)
    q_start = pl.multiple_of(qi * tq, tq)
    f_first = f_ref[0, 0, :, pl.ds(q_start, tq)][:, 0:1]

    def run(n_chunks):
        bias = lambda c: (f_first - f_ref[0, 0, :, c * ck:(c + 1) * ck]) * LOG2E
        (o,) = _causal_rows([q_ref[0]], k_ref, v_ref, s_ref, bias, qi, n_chunks, tq, ck)
        o_ref[0] = o.astype(o_ref.dtype)

    _for_each_chunk_count(qi, tq, ck, t, run)


def _attn_c_prompt(q, k, v, f_t):
    b, t, d = q.shape
    n_heads = d // HEAD
    tq, ck = _attn_tiles(t)
    return pl.pallas_call(
        functools.partial(_attn_c_kernel, tq=tq, ck=ck, t=t),
        grid=(b, n_heads, t // tq),
        in_specs=[pl.BlockSpec((1, tq, HEAD), lambda bi, h, qi: (bi, qi, h)),
                  pl.BlockSpec((1, t, HEAD), lambda bi, h, qi: (bi, 0, h)),
                  pl.BlockSpec((1, t, HEAD), lambda bi, h, qi: (bi, 0, h)),
                  pl.BlockSpec((1, 1, 1, t), lambda bi, h, qi: (bi, h, 0, 0))],
        out_specs=pl.BlockSpec((1, tq, HEAD), lambda bi, h, qi: (bi, qi, h)),
        out_shape=jax.ShapeDtypeStruct((b, t, d), BF16),
        scratch_shapes=[pltpu.VMEM((1, tq, t), F32)],
        compiler_params=_cp(3),
        name="attn_c_prompt",
    )(q, k, v, f_t)


def _cumsum_rows(x, n):
    row = lax.broadcasted_iota(jnp.int32, x.shape, 0)
    d = 1
    while d < n:
        x = x + jnp.where(row >= d, pltpu.roll(x, d, 0), 0.0)
        d *= 2
    return x


def _logf_cumsum_kernel(x_ref, o_ref, carry_ref, *, tc, n_heads):
    @pl.when(pl.program_id(1) == 0)
    def _():
        carry_ref[...] = jnp.zeros(carry_ref.shape, F32)

    f = _cumsum_rows(x_ref[0], tc) + carry_ref[...]
    carry_ref[...] = f[tc - 1:tc, :]
    o_ref[0] = f.T[0:n_heads, :]


def _logf_cumsum_t(logf_pad, n_heads, tc=128):
    b, t, _ = logf_pad.shape
    tc = min(tc, t)
    return pl.pallas_call(
        functools.partial(_logf_cumsum_kernel, tc=tc, n_heads=n_heads),
        grid=(b, t // tc),
        in_specs=[pl.BlockSpec((1, tc, LANES), lambda bi, c: (bi, c, 0))],
        out_specs=pl.BlockSpec((1, n_heads, tc), lambda bi, c: (bi, 0, c)),
        out_shape=jax.ShapeDtypeStruct((b, n_heads, t), F32),
        scratch_shapes=[pltpu.VMEM((1, LANES), F32)],
        compiler_params=_cp(2),
        name="logf_cumsum",
    )(logf_pad)


def _gla_levels():
    return [GLA_CHUNK >> (i + 1) for i in range(int(math.log2(GLA_CHUNK)))]


def _gla_masks():
    t = np.arange(GLA_CHUNK)[:, None]
    s = np.arange(GLA_CHUNK)[None, :]
    masks = []
    for b in _gla_levels():
        masks.append((t // (2 * b) == s // (2 * b)) & (t % (2 * b) >= b) & (s % (2 * b) < b))
    masks.append(t == s)
    return jnp.asarray(np.stack(masks).astype(np.float32))


def _level_reference(g_ref, hh, b):
    row = lambda r, n: jnp.broadcast_to(g_ref[hh, r:r + 1, :], (n, LANES))
    if 2 * b >= SUBLANES:
        return jnp.concatenate([row(2 * b * j + b - 1, 2 * b) for j in range(GLA_CHUNK // (2 * b))], axis=0)
    sub = lax.broadcasted_iota(jnp.int32, (SUBLANES, LANES), 0)
    pieces = []
    for j in range(GLA_CHUNK // SUBLANES):
        base = SUBLANES * j
        n_blk = SUBLANES // (2 * b)
        piece = row(base + (n_blk - 1) * 2 * b + b - 1, SUBLANES)
        for i in range(n_blk - 2, -1, -1):
            piece = jnp.where(sub < (i + 1) * 2 * b, row(base + i * 2 * b + b - 1, SUBLANES), piece)
        pieces.append(piece)
    return jnp.concatenate(pieces, axis=0)


def _lower_bound(lb_ref, layer_idx):
    x = lb_ref[...].astype(F32)
    e = jnp.exp(x - jnp.max(x, axis=0, keepdims=True))
    p = e / jnp.sum(e, axis=0, keepdims=True)
    r = lax.broadcasted_iota(jnp.int32, p.shape, 0)
    return jnp.sum(jnp.where((r >= 1) & (r <= layer_idx), p, 0.0), axis=0, keepdims=True)


def _gla_kernel(q_ref, f_ref, i_ref, g_ref, lb_ref, gn_ref, mask_ref, o_ref, s_ref,
                st_ref, gs_ref, *, layer_idx, n_chunks, hg):
    c = pl.program_id(2)
    scale = HEAD ** -0.5

    @pl.when(c == 0)
    def _():
        st_ref[...] = jnp.zeros(st_ref.shape, F32)

    lb_all = _lower_bound(lb_ref, layer_idx)
    levels = _gla_levels()
    for hh in range(hg):
        cols = slice(hh * HEAD, (hh + 1) * HEAD)
        lb = lb_all[:, cols]
        q = _silu(q_ref[:, cols]) * scale
        gate = lb + (1.0 - lb) * _sigmoid(f_ref[:, cols])
        kk = 1.0 - gate
        v = i_ref[:, cols]
        v_bf = v.astype(BF16)
        g_cum = _cumsum_rows(jnp.log(gate), GLA_CHUNK)
        gs_ref[hh] = g_cum
        st = st_ref[hh]
        o = _nt_dot((q * jnp.exp(g_cum)).astype(BF16), st.astype(BF16))

        a = jnp.sum(q * kk, axis=1, keepdims=True) * mask_ref[len(levels)]
        for lvl, b in enumerate(levels):
            d = g_cum - _level_reference(gs_ref, hh, b)
            qs = (q * jnp.exp(jnp.minimum(d, 0.0))).astype(BF16)
            ks = (kk * jnp.exp(jnp.minimum(-d, 0.0))).astype(BF16)
            a = a + _nt_dot(qs, ks) * mask_ref[lvl]
        o = o + jnp.dot(a.astype(BF16), v_bf, preferred_element_type=F32)

        g_last = gs_ref[hh, GLA_CHUNK - 1:GLA_CHUNK, :]
        k_dec = (kk * jnp.exp(g_last - g_cum)).astype(BF16)
        st_new = st * jnp.exp(g_last) + jnp.dot(v.T.astype(BF16), k_dec, preferred_element_type=F32)
        st_ref[hh] = st_new

        ms = jnp.mean(o * o, axis=-1, keepdims=True)
        y = o * lax.rsqrt(ms + EPS) * gn_ref[...]
        o_ref[:, cols] = (y * _silu(g_ref[:, cols])).astype(o_ref.dtype)

    @pl.when(c == n_chunks - 1)
    def _():
        for hh in range(hg):
            s_ref[hh] = st_ref[hh].T


def _gla_prompt(proj, lower_bounds, g_norm, layer_idx, b, t):
    m, d4 = proj.shape
    d = d4 // 4
    n_heads = d // HEAD
    hg = math.gcd(GLA_HEADS, n_heads)
    n_groups = n_heads // hg
    depth = lower_bounds.shape[0]
    n_chunks = t // GLA_CHUNK
    masks = _gla_masks()
    blk = lambda off: pl.BlockSpec((GLA_CHUNK, hg * HEAD),
                                   lambda bi, h, c: (bi * n_chunks + c, off * n_groups + h))
    return pl.pallas_call(
        functools.partial(_gla_kernel, layer_idx=layer_idx, n_chunks=n_chunks, hg=hg),
        grid=(b, n_groups, n_chunks),
        in_specs=[blk(0), blk(1), blk(2), blk(3),
                  pl.BlockSpec((depth, hg * HEAD), lambda bi, h, c: (0, h)),
                  pl.BlockSpec((1, HEAD), lambda bi, h, c: (0, 0)),
                  pl.BlockSpec(masks.shape, lambda bi, h, c: (0, 0, 0))],
        out_specs=[pl.BlockSpec((GLA_CHUNK, hg * HEAD), lambda bi, h, c: (bi * n_chunks + c, h)),
                   pl.BlockSpec((None, hg, HEAD, HEAD), lambda bi, h, c: (bi, h, 0, 0))],
        out_shape=[jax.ShapeDtypeStruct((m, d), BF16),
                   jax.ShapeDtypeStruct((b, n_heads, HEAD, HEAD), F32)],
        scratch_shapes=[pltpu.VMEM((hg, HEAD, HEAD), F32), pltpu.VMEM((hg, GLA_CHUNK, HEAD), F32)],
        compiler_params=_cp(3),
        name="gla_prompt",
    )(proj, proj, proj, proj, lower_bounds, g_norm.reshape(1, HEAD).astype(F32), masks)


def _gla_step_kernel(i_ref, g_ref, qt_ref, ft_ref, lbt_ref, gn_ref, s_in_ref, o_ref, s_out_ref,
                     *, layer_idx, n_batch):
    scale = HEAD ** -0.5
    xt = lbt_ref[...].astype(F32)
    et = jnp.exp(xt - jnp.max(xt, axis=1, keepdims=True))
    pt = et / jnp.sum(et, axis=1, keepdims=True)
    ct = lax.broadcasted_iota(jnp.int32, pt.shape, 1)
    lb_col = jnp.sum(jnp.where((ct >= 1) & (ct <= layer_idx), pt, 0.0), axis=1, keepdims=True)

    q_col = _silu(qt_ref[...]) * scale
    gate_col = lb_col + (1.0 - lb_col) * _sigmoid(ft_ref[...])
    k_col = 1.0 - gate_col
    a_all = jnp.sum(q_col * k_col, axis=0, keepdims=True)
    qg_col = q_col * gate_col
    for bi in range(n_batch):
        s_old = s_in_ref[bi]
        v_row = i_ref[bi:bi + 1, :]
        o = (jnp.sum(qg_col[:, bi:bi + 1] * s_old, axis=0, keepdims=True)
             + a_all[:, bi:bi + 1] * v_row)
        s_out_ref[bi] = gate_col[:, bi:bi + 1] * s_old + k_col[:, bi:bi + 1] * v_row
        ms = jnp.mean(o * o, axis=-1, keepdims=True)
        y = o * lax.rsqrt(ms + EPS) * gn_ref[...]
        o_ref[bi:bi + 1, :] = y * _silu(g_ref[bi:bi + 1, :])


def _gla_step(proj, state, layer_j, lower_bounds, g_norm, layer_idx):
    nb, d4 = proj.shape
    d = d4 // 4
    n_heads = d // HEAD
    depth = lower_bounds.shape[0]
    proj_t = proj.T
    lb_t = lower_bounds.T
    row = lambda off: pl.BlockSpec((nb, HEAD), lambda h: (0, off * n_heads + h))
    col = lambda off: pl.BlockSpec((HEAD, nb), lambda h: (off * n_heads + h, 0))
    return pl.pallas_call(
        functools.partial(_gla_step_kernel, layer_idx=layer_idx, n_batch=nb),
        grid=(n_heads,),
        in_specs=[row(2), row(3), col(0), col(1),
                  pl.BlockSpec((HEAD, depth), lambda h: (h, 0)),
                  pl.BlockSpec((1, HEAD), lambda h: (0, 0)),
                  pl.BlockSpec((None, nb, None, HEAD, HEAD), lambda h: (layer_j, 0, h, 0, 0))],
        out_specs=[pl.BlockSpec((nb, HEAD), lambda h: (0, h)),
                   pl.BlockSpec((nb, None, HEAD, HEAD), lambda h: (0, h, 0, 0))],
        out_shape=[jax.ShapeDtypeStruct((nb, d), F32),
                   jax.ShapeDtypeStruct((nb, n_heads, HEAD, HEAD), F32)],
        compiler_params=_cp(1),
        name="gla_step",
    )(proj, proj, proj_t, proj_t, lb_t, g_norm.reshape(1, HEAD).astype(F32), state)


def _decode_kernel(pt_ref, *refs, kind, n_pages, page, n_rows, per_step, lam_init):
    it = iter(refs)
    q_ref = next(it)
    kn_ref = next(it)
    vn_ref = next(it)
    kc_refs = [next(it) for _ in range(per_step)]
    vc_refs = [next(it) for _ in range(per_step)]
    if kind == "c":
        lfc_refs = [next(it) for _ in range(per_step)]
        lfn_ref = next(it)
    else:
        lq1, lk1, lq2, lk2 = next(it), next(it), next(it), next(it)
    o_ref = next(it)
    qbf_ref, mask_ref, m_ref, l_ref, acc_ref = next(it), next(it), next(it), next(it), next(it)
    if kind == "c":
        carry_ref = next(it)
    else:
        pair_ref = next(it)

    p = pl.program_id(1)
    scale = HEAD ** -0.5
    group = n_rows if kind == "c" else n_rows // 2
    ncols = page * group
    rows_col = lax.broadcasted_iota(jnp.int32, (n_rows, 1), 0)

    @pl.when(p == 0)
    def _():
        qbf_ref[...] = q_ref[0].astype(BF16)
        r = lax.broadcasted_iota(jnp.int32, (n_rows, ncols), 0)
        cidx = lax.broadcasted_iota(jnp.int32, (n_rows, ncols), 1)
        head_of_row = r if kind == "c" else r // 2
        mask_ref[...] = jnp.where(cidx % group == head_of_row, 0.0, NEG_INF)
        m_ref[...] = jnp.full(m_ref.shape, NEG_INF, F32)
        l_ref[...] = jnp.zeros(l_ref.shape, F32)
        acc_ref[...] = jnp.zeros(acc_ref.shape, F32)
        if kind == "c":
            carry_ref[...] = lfn_ref[0]

    lane = lax.broadcasted_iota(jnp.int32, (1, ncols), 1)
    qbf = qbf_ref[...]
    mask = mask_ref[...]
    if kind == "a":
        slope = jnp.exp2(-8.0 * ((rows_col // 2).astype(F32) + 1.0) / group)
    else:
        carry = carry_ref[...]
    scores, values = [], []
    for g in range(per_step):
        page_idx = n_pages - 1 - (p * per_step + g)
        kc_ref, vc_ref = kc_refs[g], vc_refs[g]
        if kind == "a":
            s_even = _nt_dot(qbf, kc_ref[pl.ds(0, ncols, stride=2), :].astype(BF16))
            s_odd = _nt_dot(qbf, kc_ref[pl.ds(1, ncols, stride=2), :].astype(BF16))
            s = jnp.where(rows_col % 2 == 0, s_even, s_odd) * scale
            dist = (n_pages * page - (page_idx * page + lane // group)).astype(F32)
            s = s - slope * dist + mask
            values.append(vc_ref[...].reshape(ncols, vc_ref.shape[-1]))
        else:
            s = _nt_dot(qbf, kc_ref[...].astype(BF16)) * scale
            lf2d = lfc_refs[g][...]
            flat = jnp.concatenate([lf2d[i:i + 1, :] for i in range(lf2d.shape[0])], axis=1)
            y = flat
            tot = flat
            dd = group
            while dd < ncols:
                y = y + jnp.where(lane + dd < ncols, pltpu.roll(y, ncols - dd, 1), 0.0)
                tot = tot + pltpu.roll(tot, dd, 1)
                dd *= 2
            s = s + (y - flat + carry) + mask
            carry = carry + tot
            values.append(vc_ref[...])
        scores.append(s)
    if kind == "c":
        carry_ref[...] = carry

    m_prev = m_ref[...]
    m_new = m_prev
    for s in scores:
        m_new = jnp.maximum(m_new, jnp.max(s, axis=1, keepdims=True))
    alpha = jnp.exp(m_prev - m_new)
    l_new = alpha * l_ref[...]
    acc = alpha * acc_ref[...]
    for s, v2d in zip(scores, values):
        pr = jnp.exp(s - m_new)
        l_new = l_new + jnp.sum(pr, axis=1, keepdims=True)
        acc = acc + jnp.dot(pr.astype(BF16), v2d.astype(BF16), preferred_element_type=F32)
    l_ref[...] = l_new
    acc_ref[...] = acc
    m_ref[...] = m_new

    @pl.when(p == n_pages // per_step - 1)
    def _():
        s_new = jnp.sum(q_ref[0] * kn_ref[0], axis=1, keepdims=True) * scale
        m_prev = m_ref[...]
        m_fin = jnp.maximum(m_prev, s_new)
        alpha = jnp.exp(m_prev - m_fin)
        p_new = jnp.exp(s_new - m_fin)
        l_fin = alpha * l_ref[...] + p_new
        acc = alpha * acc_ref[...] + p_new * vn_ref[0]
        o_full = acc * (1.0 / l_fin)
        if kind == "a":
            lam = _lam_value(lq1, lk1, lq2, lk2, lam_init)
            signed = o_full * jnp.where(rows_col % 2 == 0, 1.0, -lam)
            for c in range(pair_ref.shape[0]):
                pair_ref[c] = signed[:, c * LANES:(c + 1) * LANES]
                o_ref[0, :, c * LANES:(c + 1) * LANES] = (pair_ref[c, pl.ds(0, group, stride=2), :]
                                                          + pair_ref[c, pl.ds(1, group, stride=2), :])
        else:
            o_ref[0] = o_full


def _decode_attention(kind, q, k_new, v_new, cache_k, cache_v, layer_j, page_table, extras, lam_init=0.0):
    nb, n_rows, _ = q.shape
    vw = v_new.shape[-1]
    n_pages = page_table.shape[1]
    page = cache_k.shape[2] // n_rows
    group = n_rows if kind == "c" else n_rows // 2
    ncols = page * group
    per_step = math.gcd(DECODE_PAGES, n_pages)
    per_b = lambda shape: pl.BlockSpec((1,) + shape, lambda b, p, pt: (b,) + (0,) * len(shape))

    def paged(shape, g):
        return pl.BlockSpec((None, None) + shape,
                            lambda b, p, pt: (layer_j, pt[b, n_pages - 1 - (p * per_step + g)]) + (0,) * len(shape))

    steps = range(per_step)
    in_specs = ([per_b((n_rows, HEAD)), per_b((n_rows, HEAD)), per_b((n_rows, vw))]
                + [paged((page * n_rows, HEAD), g) for g in steps]
                + [paged(cache_v.shape[2:], g) for g in steps])
    args = [q, k_new, v_new] + [cache_k] * per_step + [cache_v] * per_step
    scratch = [pltpu.VMEM((n_rows, HEAD), BF16), pltpu.VMEM((n_rows, ncols), F32),
               pltpu.VMEM((n_rows, 1), F32), pltpu.VMEM((n_rows, 1), F32), pltpu.VMEM((n_rows, vw), F32)]
    if kind == "c":
        cache_logf, logf_new = extras
        in_specs += [paged(cache_logf.shape[2:], g) for g in steps] + [per_b((1, ncols))]
        args += [cache_logf] * per_step + [logf_new]
        scratch += [pltpu.VMEM((1, ncols), F32)]
    else:
        in_specs += [pl.BlockSpec((1, HEAD), lambda b, p, pt: (0, 0))] * 4
        args += [e.reshape(1, HEAD).astype(F32) for e in extras]
        scratch += [pltpu.VMEM((vw // LANES, n_rows, LANES), F32)]
    return pl.pallas_call(
        functools.partial(_decode_kernel, kind=kind, n_pages=n_pages, page=page, n_rows=n_rows,
                          per_step=per_step, lam_init=lam_init),
        grid_spec=pltpu.PrefetchScalarGridSpec(
            num_scalar_prefetch=1,
            grid=(nb, n_pages // per_step),
            in_specs=in_specs,
            out_specs=pl.BlockSpec((1, group, vw), lambda b, p, pt: (b, 0, 0)),
            scratch_shapes=scratch),
        out_shape=jax.ShapeDtypeStruct((nb, group, vw), F32),
        compiler_params=_cp(2),
        name="decode_" + kind,
    )(page_table, *args)


def _layer_a(hp, hs, xp, xs, cache_k, cache_v, j, page_table, w_in, q_norm, k_norm, lams, subln_g, w_out,
             layer_idx, b, t):
    d = xp.shape[1]
    lam_init = 0.8 - 0.6 * math.exp(-0.3 * layer_idx)
    n_layers, n_phys, page, n_rows, _ = cache_k.shape
    ck = cache_k.reshape(n_layers, n_phys, page * n_rows, HEAD)

    nb = hs.shape[0]
    (q,), (qs,) = _matmul(hp, hs, w_in, layer=j, col0=0, ncols=d, mode="gnorm", aux=q_norm, out_dtypes=(BF16,))
    (k32, k16), (ks,) = _matmul(hp, hs, w_in, layer=j, col0=d, ncols=d, mode="gnorm", aux=k_norm,
                                out_dtypes=(F32, BF16))
    (v32, v16), (vs,) = _matmul(hp, hs, w_in, layer=j, col0=2 * d, ncols=d, out_dtypes=(F32, BF16))
    o = _attn_a_prompt(q.reshape(b, t, d), k16.reshape(b, t, d), v16.reshape(b, t, d), *lams, subln_g, lam_init)
    v_rows = jnp.repeat(vs.reshape(nb, n_rows // 2, 2 * HEAD), 2, axis=1)
    os_ = _decode_attention("a", qs.reshape(nb, n_rows, HEAD), ks.reshape(nb, n_rows, HEAD), v_rows,
                            ck, cache_v, j, page_table, lams, lam_init)
    os_ = _rmsnorm(os_.reshape(nb * n_rows // 2, 2 * HEAD), subln_g, F32, post_scale=1.0 - lam_init)
    (xp,), (xs,) = _matmul(o.reshape(b * t, d), os_.reshape(nb, d), w_out, layer=j, mode="res", aux=xp, aux_s=xs)
    return xp, xs, k32, v32, ks, vs


def _layer_b(hp, hs, xp, xs, state, j, w_in, lower_bounds, g_norm, w_out, layer_idx, b, t):
    (proj,), (proj_s,) = _matmul(hp, hs, w_in, layer=j)
    o, s_p = _gla_prompt(proj, lower_bounds, g_norm, layer_idx, b, t)
    o_s, s_s = _gla_step(proj_s, state, j, lower_bounds, g_norm, layer_idx)
    (xp,), (xs,) = _matmul(o, o_s, w_out, layer=j, mode="res", aux=xp, aux_s=xs)
    return xp, xs, s_p, s_s


def _layer_c(hp, hs, xp, xs, cache_k, cache_v, cache_logf, j, page_table, w_in, b_f, q_norm, k_norm, w_out, b, t):
    d = xp.shape[1]
    n_heads = d // HEAD
    n_layers, n_phys, page = cache_k.shape[:3]
    ck = cache_k.reshape(n_layers, n_phys, page * n_heads, HEAD)
    cv = cache_v.reshape(n_layers, n_phys, page * n_heads, HEAD)
    clf = cache_logf.reshape(n_layers, n_phys, page * n_heads // LANES, LANES)
    w_f = jnp.pad(w_in[j, :, 3 * d:], ((0, 0), (0, LANES - n_heads)))
    b_pad = jnp.pad(b_f.astype(F32), (0, LANES - n_heads))

    nb = hs.shape[0]
    (q,), (qs,) = _matmul(hp, hs, w_in, layer=j, col0=0, ncols=d, mode="gnorm", aux=q_norm, out_dtypes=(BF16,))
    (k32, k16), (ks,) = _matmul(hp, hs, w_in, layer=j, col0=d, ncols=d, mode="gnorm", aux=k_norm,
                                out_dtypes=(F32, BF16))
    (v32, v16), (vs,) = _matmul(hp, hs, w_in, layer=j, col0=2 * d, ncols=d, out_dtypes=(F32, BF16))
    (logf,), (lfs_pad,) = _matmul(hp, hs, w_f, mode="logsig", aux=b_pad)
    lfs = lfs_pad[:, :n_heads]
    f_t = _logf_cumsum_t(logf.reshape(b, t, LANES), n_heads)
    o = _attn_c_prompt(q.reshape(b, t, d), k16.reshape(b, t, d), v16.reshape(b, t, d),
                       f_t.reshape(b, n_heads, 1, t))
    lf_new = jnp.tile(lfs, (1, page)).reshape(nb, 1, page * n_heads)
    os_ = _decode_attention("c", qs.reshape(nb, n_heads, HEAD), ks.reshape(nb, n_heads, HEAD),
                            vs.reshape(nb, n_heads, HEAD), ck, cv, j, page_table, (clf, lf_new))
    (xp,), (xs,) = _matmul(o.reshape(b * t, d), os_.reshape(nb, d), w_out, layer=j, mode="res", aux=xp, aux_s=xs)
    return xp, xs, k32, v32, logf[:, :n_heads], ks, vs, lfs


def _ffn(xp, xs, g, w_gu, w_down, layer):
    d_ff = w_down.shape[1]
    hp = _rmsnorm(xp, g, BF16)
    hs = _rmsnorm(xs, g, F32)
    (act,), (act_s,) = _matmul(hp, hs, w_gu, layer=layer, col0=0, ncols=d_ff, col1=d_ff, mode="swiglu",
                               out_dtypes=(BF16,), out_dtypes_s=(BF16,), tn=512)
    (xp,), (xs,) = _matmul(act, act_s, w_down, layer=layer, mode="res", aux=xp, aux_s=xs, tn=512)
    return xp, xs


def kernel(x_prompt, x_sample, cache_k_a, cache_v_a, state_s_b, cache_k_c, cache_v_c, cache_logf_c, page_table, norm1_g, norm2_g, a_w_in, a_q_norm, a_k_norm, a_lam_q1, a_lam_k1, a_lam_q2, a_lam_k2, a_subln_g, a_w_out, b_w_in, b_lower_bounds, b_out_norm_g, b_w_out, c_w_in, c_b_f, c_q_norm, c_k_norm, c_w_out, ffn_w_gu, ffn_w_down):
    b, t, d = x_prompt.shape
    nb, ds, _ = x_sample.shape
    assert ds == 1 and d % (2 * HEAD) == 0 and t % GLA_CHUNK == 0
    depth = norm1_g.shape[0]
    xp = x_prompt.reshape(b * t, d)
    xs = x_sample.reshape(nb, d)
    h_a, h_c = d // (2 * HEAD), d // HEAD
    outs = {name: [] for name in ("ka_p", "va_p", "ka_s", "va_s", "sb_p", "sb_s",
                                  "kc_p", "vc_p", "lc_p", "kc_s", "vc_s", "lc_s")}
    for i in range(depth):
        j = i // 3
        hp = _rmsnorm(xp, norm1_g[i], BF16)
        hs = _rmsnorm(xs, norm1_g[i], F32)
        kind = i % 3
        if kind == 0:
            lams = (a_lam_q1[j], a_lam_k1[j], a_lam_q2[j], a_lam_k2[j])
            xp, xs, k1, v1, k2, v2 = _layer_a(hp, hs, xp, xs, cache_k_a, cache_v_a, j, page_table, a_w_in,
                                              a_q_norm[j], a_k_norm[j], lams, a_subln_g[j], a_w_out, i, b, t)
            outs["ka_p"].append(k1.reshape(b, t, 2 * h_a, HEAD))
            outs["va_p"].append(v1.reshape(b, t, h_a, 2 * HEAD))
            outs["ka_s"].append(k2.reshape(nb, 1, 2 * h_a, HEAD))
            outs["va_s"].append(v2.reshape(nb, 1, h_a, 2 * HEAD))
        elif kind == 1:
            xp, xs, s1, s2 = _layer_b(hp, hs, xp, xs, state_s_b, j, b_w_in, b_lower_bounds,
                                      b_out_norm_g[j], b_w_out, i, b, t)
            outs["sb_p"].append(s1)
            outs["sb_s"].append(s2)
        else:
            xp, xs, k1, v1, l1, k2, v2, l2 = _layer_c(hp, hs, xp, xs, cache_k_c, cache_v_c, cache_logf_c, j,
                                                      page_table, c_w_in, c_b_f[j], c_q_norm[j],
                                                      c_k_norm[j], c_w_out, b, t)
            outs["kc_p"].append(k1.reshape(b, t, h_c, HEAD))
            outs["vc_p"].append(v1.reshape(b, t, h_c, HEAD))
            outs["lc_p"].append(l1.reshape(b, t, h_c))
            outs["kc_s"].append(k2.reshape(nb, 1, h_c, HEAD))
            outs["vc_s"].append(v2.reshape(nb, 1, h_c, HEAD))
            outs["lc_s"].append(l2.reshape(nb, 1, h_c))
        xp, xs = _ffn(xp, xs, norm2_g[i], ffn_w_gu, ffn_w_down, i)
    st = lambda name: jnp.stack(outs[name])
    return (xp.reshape(b, t, d), xs.reshape(nb, 1, d), st("ka_p"), st("va_p"), st("ka_s"), st("va_s"),
            st("sb_p"), st("sb_s"), st("kc_p"), st("vc_p"), st("lc_p"), st("kc_s"), st("vc_s"), st("lc_s"))
```

```python
import functools
import math

import numpy as np
import jax
import jax.numpy as jnp
from jax import lax
from jax.experimental import pallas as pl
from jax.experimental.pallas import tpu as pltpu

F32 = jnp.float32
BF16 = jnp.bfloat16
EPS = 1e-6
LANES = 128
SUBLANES = 8
VMEM_LIMIT = 56 * 1024 * 1024
MM_VMEM_BUDGET = 48 * 1024 * 1024
HEAD = 128
GLA_CHUNK = 128
ATT_TQ_A = 512
ATT_TQ_C = 256
ATT_HEADS_C = 2
ATT_CK = 512
DECODE_PAGES = 8
GLA_HEADS = 4
LOG2E = 1.4426950408889634
NEG_INF = float("-inf")


def _cp(n_axes):
    return pltpu.CompilerParams(dimension_semantics=("arbitrary",) * n_axes,
                                vmem_limit_bytes=VMEM_LIMIT)


def _sigmoid(x):
    return 1.0 / (1.0 + jnp.exp(-x))


def _silu(x):
    return x * _sigmoid(x)


def _log_sigmoid(x):
    return jnp.minimum(x, 0.0) - jnp.log1p(jnp.exp(-jnp.abs(x)))


def _nt_dot(a, b):
    return lax.dot_general(a, b, (((1,), (1,)), ((), ())), preferred_element_type=F32)


def _rmsnorm_kernel(x_ref, g_ref, o_ref, *, post_scale):
    x = x_ref[...].astype(F32)
    ms = jnp.mean(x * x, axis=-1, keepdims=True)
    y = x * lax.rsqrt(ms + EPS) * g_ref[...]
    if post_scale != 1.0:
        y = y * post_scale
    o_ref[...] = y.astype(o_ref.dtype)


def _rmsnorm(x, g, out_dtype, post_scale=1.0):
    rows, d = x.shape
    tr = min(rows, 512)
    return pl.pallas_call(
        functools.partial(_rmsnorm_kernel, post_scale=post_scale),
        grid=(rows // tr,),
        in_specs=[pl.BlockSpec((tr, d), lambda i: (i, 0)),
                  pl.BlockSpec((1, d), lambda i: (0, 0))],
        out_specs=pl.BlockSpec((tr, d), lambda i: (i, 0)),
        out_shape=jax.ShapeDtypeStruct((rows, d), out_dtype),
        compiler_params=_cp(1),
        name="rmsnorm",
    )(x, g.reshape(1, d).astype(F32))


def _mm_epilogue(acc, up, aux, out_refs, mode, tn):
    if mode == "swiglu":
        acc = _silu(acc) * up
    elif mode == "res":
        acc = aux + acc
    elif mode == "logsig":
        acc = _log_sigmoid(acc + aux)
    if mode == "gnorm":
        for c in range(tn // HEAD):
            blk = acc[:, c * HEAD:(c + 1) * HEAD]
            ms = jnp.mean(blk * blk, axis=-1, keepdims=True)
            y = blk * lax.rsqrt(ms + EPS) * aux
            for o_ref in out_refs:
                o_ref[:, c * HEAD:(c + 1) * HEAD] = y.astype(o_ref.dtype)
    else:
        for o_ref in out_refs:
            o_ref[...] = acc.astype(o_ref.dtype)


def _mm_kernel(*refs, mode, n_out, n_out_s, tn):
    dual = mode == "swiglu"
    has_aux = mode in ("gnorm", "res", "logsig")
    it = iter(refs)
    x_ref = next(it)
    w_ref = next(it)
    w2_ref = next(it) if dual else None
    aux_ref = next(it) if has_aux else None
    xs_ref = next(it)
    aux_s_ref = next(it) if mode == "res" else aux_ref
    out_refs = [next(it) for _ in range(n_out)]
    out_s_refs = [next(it) for _ in range(n_out_s)]
    wbf_ref = next(it)
    wbf2_ref = next(it) if dual else None

    @pl.when(pl.program_id(1) == 0)
    def _():
        wbf_ref[...] = w_ref[...].astype(BF16)
        if dual:
            wbf2_ref[...] = w2_ref[...].astype(BF16)
        xs = xs_ref[...].astype(BF16)
        acc_s = jnp.dot(xs, wbf_ref[...], preferred_element_type=F32)
        up_s = jnp.dot(xs, wbf2_ref[...], preferred_element_type=F32) if dual else None
        _mm_epilogue(acc_s, up_s, aux_s_ref[...] if has_aux else None, out_s_refs, mode, tn)

    x = x_ref[...].astype(BF16)
    acc = jnp.dot(x, wbf_ref[...], preferred_element_type=F32)
    up = jnp.dot(x, wbf2_ref[...], preferred_element_type=F32) if dual else None
    _mm_epilogue(acc, up, aux_ref[...] if has_aux else None, out_refs, mode, tn)


def _matmul(x, xs, w, *, layer=0, col0=0, ncols=None, mode="plain", out_dtypes=(F32,), out_dtypes_s=(F32,),
            aux=None, aux_s=None, col1=None, tn=1024):
    if w.ndim == 2:
        w = w[None]
    m, k = x.shape
    ms = xs.shape[0]
    ncols = w.shape[2] - col0 if ncols is None else ncols
    tn = min(tn, ncols)
    n_w = 2 if mode == "swiglu" else 1

    def vmem_estimate(rows):
        blocks = rows * k * x.dtype.itemsize + n_w * k * tn * 4
        blocks += sum(rows * tn * jnp.dtype(dt).itemsize for dt in out_dtypes)
        blocks += rows * tn * 4 if mode == "res" else 0
        return 2 * blocks + n_w * k * tn * 2

    tm = next((r for r in (1024, 512, 256) if m % r == 0 and vmem_estimate(r) <= MM_VMEM_BUDGET), min(m, 128))
    assert m % tm == 0 and ncols % tn == 0 and col0 % tn == 0
    nb0 = col0 // tn
    in_specs = [pl.BlockSpec((tm, k), lambda n, i: (i, 0)),
                pl.BlockSpec((None, k, tn), lambda n, i: (layer, 0, n + nb0))]
    args = [x, w]
    scratch = [pltpu.VMEM((k, tn), BF16)]
    if mode == "swiglu":
        assert col1 % tn == 0
        nb1 = col1 // tn
        in_specs.append(pl.BlockSpec((None, k, tn), lambda n, i: (layer, 0, n + nb1)))
        args.append(w)
        scratch.append(pltpu.VMEM((k, tn), BF16))
    if mode == "gnorm":
        in_specs.append(pl.BlockSpec((1, HEAD), lambda n, i: (0, 0)))
        args.append(aux.reshape(1, HEAD).astype(F32))
    elif mode == "res":
        in_specs.append(pl.BlockSpec((tm, tn), lambda n, i: (i, n)))
        args.append(aux)
    elif mode == "logsig":
        in_specs.append(pl.BlockSpec((1, tn), lambda n, i: (0, n)))
        args.append(aux.reshape(1, ncols).astype(F32))
    in_specs.append(pl.BlockSpec((ms, k), lambda n, i: (0, 0)))
    args.append(xs)
    if mode == "res":
        in_specs.append(pl.BlockSpec((ms, tn), lambda n, i: (0, n)))
        args.append(aux_s)
    outs = pl.pallas_call(
        functools.partial(_mm_kernel, mode=mode, n_out=len(out_dtypes), n_out_s=len(out_dtypes_s), tn=tn),
        grid=(ncols // tn, m // tm),
        in_specs=in_specs,
        out_specs=([pl.BlockSpec((tm, tn), lambda n, i: (i, n)) for _ in out_dtypes]
                   + [pl.BlockSpec((ms, tn), lambda n, i: (0, n)) for _ in out_dtypes_s]),
        out_shape=([jax.ShapeDtypeStruct((m, ncols), dt) for dt in out_dtypes]
                   + [jax.ShapeDtypeStruct((ms, ncols), dt) for dt in out_dtypes_s]),
        scratch_shapes=scratch,
        compiler_params=_cp(2),
        name="mm_" + mode,
    )(*args)
    return tuple(outs[:len(out_dtypes)]), tuple(outs[len(out_dtypes):])


def _lam_value(lq1, lk1, lq2, lk2, lam_init):
    a = jnp.sum(lq1[...] * lk1[...], axis=-1, keepdims=True)
    b = jnp.sum(lq2[...] * lk2[...], axis=-1, keepdims=True)
    return jnp.exp(a) - jnp.exp(b) + lam_init


def _causal_rows(qs, k_ref, v_ref, s_ref, bias_of_chunk, qi, n_chunks, tq, ck, value_cols):
    cs = (HEAD ** -0.5) * LOG2E
    n_maps = len(qs)
    row = lax.broadcasted_iota(jnp.int32, (tq, ck), 0)
    col = lax.broadcasted_iota(jnp.int32, (tq, ck), 1)
    m_part = [jnp.full((tq, LANES), NEG_INF, F32) for _ in range(n_maps)]
    for c in range(n_chunks):
        k = k_ref[0, c * ck:(c + 1) * ck, :]
        for j in range(n_maps):
            s = _nt_dot(qs[j], k[:, j * HEAD:(j + 1) * HEAD]) * cs + bias_of_chunk(c, j)
            if c == n_chunks - 1:
                s = jnp.where(row - col >= c * ck - qi * tq, s, NEG_INF)
            s_ref[j, :, c * ck:(c + 1) * ck] = s
            for g in range(ck // LANES):
                m_part[j] = jnp.maximum(m_part[j], s[:, g * LANES:(g + 1) * LANES])

    outs = []
    for j in range(n_maps):
        m = jnp.max(m_part[j], axis=1, keepdims=True)
        l_part = jnp.zeros((tq, LANES), F32)
        o = None
        for c in range(n_chunks):
            p = jnp.exp2(s_ref[j, :, c * ck:(c + 1) * ck] - m)
            for g in range(ck // LANES):
                l_part = l_part + p[:, g * LANES:(g + 1) * LANES]
            pv = jnp.dot(p.astype(BF16), v_ref[0, c * ck:(c + 1) * ck, value_cols[j]],
                         preferred_element_type=F32)
            o = pv if o is None else o + pv
        outs.append(o * (1.0 / jnp.sum(l_part, axis=1, keepdims=True)))
    return outs


def _for_each_chunk_count(qi, tq, ck, t, fn):
    per = ck // tq
    for n_chunks in range(1, t // ck + 1):
        pl.when(qi // per == n_chunks - 1)(functools.partial(fn, n_chunks))


def _attn_a_kernel(q_ref, k_ref, v_ref, lq1, lk1, lq2, lk2, sg_ref, o_ref, s_ref,
                   *, tq, ck, t, n_heads, lam_init):
    h = pl.program_id(1)
    qi = pl.program_id(2)
    slope2 = LOG2E * jnp.exp2(-8.0 * (jnp.full((1, 1), h, jnp.int32).astype(F32) + 1.0) / n_heads)
    lam = _lam_value(lq1, lk1, lq2, lk2, lam_init)
    col = lax.broadcasted_iota(jnp.int32, (1, ck), 1)
    q = q_ref[0]
    qs = [q[:, j * HEAD:(j + 1) * HEAD] for j in range(2)]

    def run(n_chunks):
        bias = lambda c, j: slope2 * (col + (c * ck - qi * tq)).astype(F32)
        both = slice(0, 2 * HEAD)
        o1, o2 = _causal_rows(qs, k_ref, v_ref, s_ref, bias, qi, n_chunks, tq, ck, (both, both))
        o = o1 - lam * o2
        ms =jnp.mean(o * o, axis=-1, keepdims=True)
        y = o * lax.rsqrt(ms + EPS) * sg_ref[...]
        o_ref[0] = (y * (1.0 - lam_init)).astype(o_ref.dtype)

    _for_each_chunk_count(qi, tq, ck, t, run)


def _attn_tiles(t, tq):
    tq = min(tq, t)
    ck = min(ATT_CK, t)
    assert t % ck == 0 and ck % tq == 0
    return tq, ck


def _attn_a_prompt(q, k, v, lq1, lk1, lq2, lk2, subln_g, lam_init):
    b, t, d = q.shape
    n_heads = d // (2 * HEAD)
    tq, ck = _attn_tiles(t, ATT_TQ_A)
    vec = lambda a: a.reshape(1, -1).astype(F32)
    small = lambda w: pl.BlockSpec((1, w), lambda bi, h, qi: (0, 0))
    return pl.pallas_call(
        functools.partial(_attn_a_kernel, tq=tq, ck=ck, t=t, n_heads=n_heads, lam_init=lam_init),
        grid=(b, n_heads, t // tq),
        in_specs=[pl.BlockSpec((1, tq, 2 * HEAD), lambda bi, h, qi: (bi, qi, h)),
                  pl.BlockSpec((1, t, 2 * HEAD), lambda bi, h, qi: (bi, 0, h)),
                  pl.BlockSpec((1, t, 2 * HEAD), lambda bi, h, qi: (bi, 0, h)),
                  small(HEAD), small(HEAD), small(HEAD), small(HEAD), small(2 * HEAD)],
        out_specs=pl.BlockSpec((1, tq, 2 * HEAD), lambda bi, h, qi: (bi, qi, h)),
        out_shape=jax.ShapeDtypeStruct((b, t, d), BF16),
        scratch_shapes=[pltpu.VMEM((2, tq, t), F32)],
        compiler_params=_cp(3),
        name="attn_a_prompt",
    )(q, k, v, vec(lq1), vec(lk1), vec(lq2), vec(lk2), vec(subln_g))


def _attn_c_kernel(q_ref, k_ref, v_ref, f_ref, o_ref, s_ref, *, tq, ck, t, hg):
    qi = pl.program_id(2)
    q_start = pl.multiple_of(qi * tq, tq)
    f_first = [f_ref[0, j, :, pl.ds(q_start, tq)][:, 0:1] for j in range(hg)]
    cols = [slice(j * HEAD, (j + 1) * HEAD) for j in range(hg)]
    q = q_ref[0]

    def run(n_chunks):
        bias = lambda c, j: (f_first[j] - f_ref[0, j, :, c * ck:(c + 1) * ck]) * LOG2E
        outs = _causal_rows([q[:, cj] for cj in cols], k_ref, v_ref, s_ref, bias, qi, n_chunks, tq, ck, cols)
        for cj, o in zip(cols, outs):
            o_ref[0, :, cj] = o.astype(o_ref.dtype)

    _for_each_chunk_count(qi, tq, ck, t, run)


def _attn_c_prompt(q, k, v, f_t):
    b, t, d = q.shape
    n_heads = d // HEAD
    hg = math.gcd(ATT_HEADS_C, n_heads)
    tq, ck = _attn_tiles(t, ATT_TQ_C)
    return pl.pallas_call(
        functools.partial(_attn_c_kernel, tq=tq, ck=ck, t=t, hg=hg),
        grid=(b, n_heads // hg, t // tq),
        in_specs=[pl.BlockSpec((1, tq, hg * HEAD), lambda bi, h, qi: (bi, qi, h)),
                  pl.BlockSpec((1, t, hg * HEAD), lambda bi, h, qi: (bi, 0, h)),
                  pl.BlockSpec((1, t, hg * HEAD), lambda bi, h, qi: (bi, 0, h)),
                  pl.BlockSpec((1, hg, 1, t), lambda bi, h, qi: (bi, h, 0, 0))],
        out_specs=pl.BlockSpec((1, tq, hg * HEAD), lambda bi, h, qi: (bi, qi, h)),
        out_shape=jax.ShapeDtypeStruct((b, t, d), BF16),
        scratch_shapes=[pltpu.VMEM((hg, tq, t), F32)],
        compiler_params=_cp(3),
        name="attn_c_prompt",
    )(q, k, v, f_t)


def _cumsum_rows(x, n):
    row = lax.broadcasted_iota(jnp.int32, x.shape, 0)
    d = 1
    while d < n:
        x = x + jnp.where(row >= d, pltpu.roll(x, d, 0), 0.0)
        d *= 2
    return x


def _logf_cumsum_kernel(x_ref, o_ref, carry_ref, *, tc, n_heads):
    @pl.when(pl.program_id(1) == 0)
    def _():
        carry_ref[...] = jnp.zeros(carry_ref.shape, F32)

    f = _cumsum_rows(x_ref[0], tc) + carry_ref[...]
    carry_ref[...] = f[tc - 1:tc, :]
    o_ref[0] = f.T[0:n_heads, :]


def _logf_cumsum_t(logf_pad, n_heads, tc=128):
    b, t, _ = logf_pad.shape
    tc = min(tc, t)
    return pl.pallas_call(
        functools.partial(_logf_cumsum_kernel, tc=tc, n_heads=n_heads),
        grid=(b, t // tc),
        in_specs=[pl.BlockSpec((1, tc, LANES), lambda bi, c: (bi, c, 0))],
        out_specs=pl.BlockSpec((1, n_heads, tc), lambda bi, c: (bi, 0, c)),
        out_shape=jax.ShapeDtypeStruct((b, n_heads, t), F32),
        scratch_shapes=[pltpu.VMEM((1, LANES), F32)],
        compiler_params=_cp(2),
        name="logf_cumsum",
    )(logf_pad)


def _gla_levels():
    return [GLA_CHUNK >> (i + 1) for i in range(int(math.log2(GLA_CHUNK)))]


def _gla_masks():
    t = np.arange(GLA_CHUNK)[:, None]
    s = np.arange(GLA_CHUNK)[None, :]
    masks = []
    for b in _gla_levels():
        masks.append((t // (2 * b) == s // (2 * b)) & (t % (2 * b) >= b) & (s % (2 * b) < b))
    masks.append(t == s)
    return jnp.asarray(np.stack(masks).astype(np.float32))


def _level_reference(g_ref, hh, b):
    row = lambda r, n: jnp.broadcast_to(g_ref[hh, r:r + 1, :], (n, LANES))
    if 2 * b >= SUBLANES:
        return jnp.concatenate([row(2 * b * j + b - 1, 2 * b) for j in range(GLA_CHUNK // (2 * b))], axis=0)
    sub = lax.broadcasted_iota(jnp.int32, (SUBLANES, LANES), 0)
    pieces = []
    for j in range(GLA_CHUNK // SUBLANES):
        base = SUBLANES * j
        n_blk = SUBLANES // (2 * b)
        piece = row(base + (n_blk - 1) * 2 * b + b - 1, SUBLANES)
        for i in range(n_blk - 2, -1, -1):
            piece = jnp.where(sub < (i + 1) * 2 * b, row(base + i * 2 * b + b - 1, SUBLANES), piece)
        pieces.append(piece)
    return jnp.concatenate(pieces, axis=0)


def _lower_bound(lb_ref, layer_idx):
    x = lb_ref[...].astype(F32)
    e = jnp.exp(x - jnp.max(x, axis=0, keepdims=True))
    p = e / jnp.sum(e, axis=0, keepdims=True)
    r = lax.broadcasted_iota(jnp.int32, p.shape, 0)
    return jnp.sum(jnp.where((r >= 1) & (r <= layer_idx), p, 0.0), axis=0, keepdims=True)


def _gla_kernel(q_ref, f_ref, i_ref, g_ref, lb_ref, gn_ref, mask_ref, o_ref, s_ref,
                st_ref, gs_ref, *, layer_idx, n_chunks, hg):
    c = pl.program_id(2)
    scale = HEAD ** -0.5

    @pl.when(c == 0)
    def _():
        st_ref[...] = jnp.zeros(st_ref.shape, F32)

    lb_all = _lower_bound(lb_ref, layer_idx)
    levels = _gla_levels()
    for hh in range(hg):
        cols = slice(hh * HEAD, (hh + 1) * HEAD)
        lb = lb_all[:, cols]
        q = _silu(q_ref[:, cols]) * scale
        gate = lb + (1.0 - lb) * _sigmoid(f_ref[:, cols])
        kk = 1.0 - gate
        v = i_ref[:, cols]
        v_bf = v.astype(BF16)
        g_cum = _cumsum_rows(jnp.log(gate), GLA_CHUNK)
        gs_ref[hh] = g_cum
        st = st_ref[hh]
        o = _nt_dot((q * jnp.exp(g_cum)).astype(BF16), st.astype(BF16))

        a = jnp.sum(q * kk, axis=1, keepdims=True) * mask_ref[len(levels)]
        for lvl, b in enumerate(levels):
            d = g_cum - _level_reference(gs_ref, hh, b)
            qs = (q * jnp.exp(jnp.minimum(d, 0.0))).astype(BF16)
            ks = (kk * jnp.exp(jnp.minimum(-d, 0.0))).astype(BF16)
            a = a + _nt_dot(qs, ks) * mask_ref[lvl]
        o = o + jnp.dot(a.astype(BF16), v_bf, preferred_element_type=F32)

        g_last = gs_ref[hh, GLA_CHUNK - 1:GLA_CHUNK, :]
        k_dec = (kk * jnp.exp(g_last - g_cum)).astype(BF16)
        st_new = st * jnp.exp(g_last) + jnp.dot(v.T.astype(BF16), k_dec, preferred_element_type=F32)
        st_ref[hh] = st_new

        ms = jnp.mean(o * o, axis=-1, keepdims=True)
        y = o * lax.rsqrt(ms + EPS) * gn_ref[...]
        o_ref[:, cols] = (y * _silu(g_ref[:, cols])).astype(o_ref.dtype)

    @pl.when(c == n_chunks - 1)
    def _():
        for hh in range(hg):
            s_ref[hh] = st_ref[hh].T


def _gla_prompt(proj, lower_bounds, g_norm, layer_idx, b, t):
    m, d4 = proj.shape
    d = d4 // 4
    n_heads = d // HEAD
    hg = math.gcd(GLA_HEADS, n_heads)
    n_groups = n_heads // hg
    depth = lower_bounds.shape[0]
    n_chunks = t // GLA_CHUNK
    masks = _gla_masks()
    blk = lambda off: pl.BlockSpec((GLA_CHUNK, hg * HEAD),
                                   lambda bi, h, c: (bi * n_chunks + c, off * n_groups + h))
    return pl.pallas_call(
        functools.partial(_gla_kernel, layer_idx=layer_idx, n_chunks=n_chunks, hg=hg),
        grid=(b, n_groups, n_chunks),
        in_specs=[blk(0), blk(1), blk(2), blk(3),
                  pl.BlockSpec((depth, hg * HEAD), lambda bi, h, c: (0, h)),
                  pl.BlockSpec((1, HEAD), lambda bi, h, c: (0, 0)),
                  pl.BlockSpec(masks.shape, lambda bi, h, c: (0, 0, 0))],
        out_specs=[pl.BlockSpec((GLA_CHUNK, hg * HEAD), lambda bi, h, c: (bi * n_chunks + c, h)),
                   pl.BlockSpec((None, hg, HEAD, HEAD), lambda bi, h, c: (bi, h, 0, 0))],
        out_shape=[jax.ShapeDtypeStruct((m, d), BF16),
                   jax.ShapeDtypeStruct((b, n_heads, HEAD, HEAD), F32)],
        scratch_shapes=[pltpu.VMEM((hg, HEAD, HEAD), F32), pltpu.VMEM((hg, GLA_CHUNK, HEAD), F32)],
        compiler_params=_cp(3),
        name="gla_prompt",
    )(proj, proj, proj, proj, lower_bounds, g_norm.reshape(1, HEAD).astype(F32), masks)


def _gla_step_kernel(i_ref, g_ref, qt_ref, ft_ref, lbt_ref, gn_ref, s_in_ref, o_ref, s_out_ref,
                     *, layer_idx, n_batch):
    scale = HEAD ** -0.5
    xt = lbt_ref[...].astype(F32)
    et = jnp.exp(xt - jnp.max(xt, axis=1, keepdims=True))
    pt = et / jnp.sum(et, axis=1, keepdims=True)
    ct = lax.broadcasted_iota(jnp.int32, pt.shape, 1)
    lb_col = jnp.sum(jnp.where((ct >= 1) & (ct <= layer_idx), pt, 0.0), axis=1, keepdims=True)

    q_col = _silu(qt_ref[...]) * scale
    gate_col = lb_col + (1.0 - lb_col) * _sigmoid(ft_ref[...])
    k_col = 1.0 - gate_col
    a_all = jnp.sum(q_col * k_col, axis=0, keepdims=True)
    qg_col = q_col * gate_col
    for bi in range(n_batch):
        s_old = s_in_ref[bi]
        v_row = i_ref[bi:bi + 1, :]
        o = (jnp.sum(qg_col[:, bi:bi + 1] * s_old, axis=0, keepdims=True)
             + a_all[:, bi:bi + 1] * v_row)
        s_out_ref[bi] = gate_col[:, bi:bi + 1] * s_old + k_col[:, bi:bi + 1] * v_row
        ms = jnp.mean(o * o, axis=-1, keepdims=True)
        y = o * lax.rsqrt(ms + EPS) * gn_ref[...]
        o_ref[bi:bi + 1, :] = y * _silu(g_ref[bi:bi + 1, :])


def _gla_step(proj, state, layer_j, lower_bounds, g_norm, layer_idx):
    nb, d4 = proj.shape
    d = d4 // 4
    n_heads = d // HEAD
    depth = lower_bounds.shape[0]
    proj_t = proj.T
    lb_t = lower_bounds.T
    row = lambda off: pl.BlockSpec((nb, HEAD), lambda h: (0, off * n_heads + h))
    col = lambda off: pl.BlockSpec((HEAD, nb), lambda h: (off * n_heads + h, 0))
    return pl.pallas_call(
        functools.partial(_gla_step_kernel, layer_idx=layer_idx, n_batch=nb),
        grid=(n_heads,),
        in_specs=[row(2), row(3), col(0), col(1),
                  pl.BlockSpec((HEAD, depth), lambda h: (h, 0)),
                  pl.BlockSpec((1, HEAD), lambda h: (0, 0)),
                  pl.BlockSpec((None, nb, None, HEAD, HEAD), lambda h: (layer_j, 0, h, 0, 0))],
        out_specs=[pl.BlockSpec((nb, HEAD), lambda h: (0, h)),
                   pl.BlockSpec((nb, None, HEAD, HEAD), lambda h: (0, h, 0, 0))],
        out_shape=[jax.ShapeDtypeStruct((nb, d), F32),
                   jax.ShapeDtypeStruct((nb, n_heads, HEAD, HEAD), F32)],
        compiler_params=_cp(1),
        name="gla_step",
    )(proj, proj, proj_t, proj_t, lb_t, g_norm.reshape(1, HEAD).astype(F32), state)


def _decode_kernel(pt_ref, *refs, kind, n_pages, page, n_rows, per_step, lam_init):
    it = iter(refs)
    q_ref = next(it)
    kn_ref = next(it)
    vn_ref = next(it)
    kc_refs = [next(it) for _ in range(per_step)]
    vc_refs = [next(it) for _ in range(per_step)]
    if kind == "c":
        lfc_refs = [next(it) for _ in range(per_step)]
        lfn_ref = next(it)
    else:
        lq1, lk1, lq2, lk2 = next(it), next(it), next(it), next(it)
    o_ref = next(it)
    qbf_ref, mask_ref, m_ref, l_ref, acc_ref = next(it), next(it), next(it), next(it), next(it)
    if kind == "c":
        carry_ref = next(it)
    else:
        pair_ref = next(it)

    p = pl.program_id(1)
    scale = HEAD ** -0.5
    group = n_rows if kind == "c" else n_rows // 2
    ncols = page * group
    rows_col = lax.broadcasted_iota(jnp.int32, (n_rows, 1), 0)

    @pl.when(p == 0)
    def _():
        qbf_ref[...] = q_ref[0].astype(BF16)
        r = lax.broadcasted_iota(jnp.int32, (n_rows, ncols), 0)
        cidx = lax.broadcasted_iota(jnp.int32, (n_rows, ncols), 1)
        head_of_row = r if kind == "c" else r // 2
        mask_ref[...] = jnp.where(cidx % group == head_of_row, 0.0, NEG_INF)
        m_ref[...] = jnp.full(m_ref.shape, NEG_INF, F32)
        l_ref[...] = jnp.zeros(l_ref.shape, F32)
        acc_ref[...] = jnp.zeros(acc_ref.shape, F32)
        if kind == "c":
            carry_ref[...] = lfn_ref[0]

    lane = lax.broadcasted_iota(jnp.int32, (1, ncols), 1)
    qbf = qbf_ref[...]
    mask = mask_ref[...]
    if kind == "a":
        slope = jnp.exp2(-8.0 * ((rows_col // 2).astype(F32) + 1.0) / group)
    else:
        carry = carry_ref[...]
    scores, values = [], []
    for g in range(per_step):
        page_idx = n_pages - 1 - (p * per_step + g)
        kc_ref, vc_ref = kc_refs[g], vc_refs[g]
        if kind == "a":
            s_even = _nt_dot(qbf, kc_ref[pl.ds(0, ncols, stride=2), :].astype(BF16))
            s_odd = _nt_dot(qbf, kc_ref[pl.ds(1, ncols, stride=2), :].astype(BF16))
            s = jnp.where(rows_col % 2 == 0, s_even, s_odd) * scale
            dist = (n_pages * page - (page_idx * page + lane // group)).astype(F32)
            s = s - slope * dist + mask
            values.append(vc_ref[...].reshape(ncols, vc_ref.shape[-1]))
        else:
            s = _nt_dot(qbf, kc_ref[...].astype(BF16)) * scale
            lf2d = lfc_refs[g][...]
            flat = jnp.concatenate([lf2d[i:i + 1, :] for i in range(lf2d.shape[0])], axis=1)
            y = flat
            tot = flat
            dd = group
            while dd < ncols:
                y = y + jnp.where(lane + dd < ncols, pltpu.roll(y, ncols - dd, 1), 0.0)
                tot = tot + pltpu.roll(tot, dd, 1)
                dd *= 2
            s = s + (y - flat + carry) + mask
            carry = carry + tot
            values.append(vc_ref[...])
        scores.append(s)
    if kind == "c":
        carry_ref[...] = carry

    m_prev = m_ref[...]
    m_new = m_prev
    for s in scores:
        m_new = jnp.maximum(m_new, jnp.max(s, axis=1, keepdims=True))
    alpha = jnp.exp(m_prev - m_new)
    l_new = alpha * l_ref[...]
    acc = alpha * acc_ref[...]
    for s, v2d in zip(scores, values):
        pr = jnp.exp(s - m_new)
        l_new = l_new + jnp.sum(pr, axis=1, keepdims=True)
        acc = acc + jnp.dot(pr.astype(BF16), v2d.astype(BF16), preferred_element_type=F32)
    l_ref[...] = l_new
    acc_ref[...] = acc
    m_ref[...] = m_new

    @pl.when(p == n_pages // per_step - 1)
    def _():
        s_new = jnp.sum(q_ref[0] * kn_ref[0], axis=1, keepdims=True) * scale
        m_prev = m_ref[...]
        m_fin = jnp.maximum(m_prev, s_new)
        alpha = jnp.exp(m_prev - m_fin)
        p_new = jnp.exp(s_new - m_fin)
        l_fin = alpha * l_ref[...] + p_new
        acc = alpha * acc_ref[...] + p_new * vn_ref[0]
        o_full = acc * (1.0 / l_fin)
        if kind == "a":
            lam = _lam_value(lq1, lk1, lq2, lk2, lam_init)
            signed = o_full * jnp.where(rows_col % 2 == 0, 1.0, -lam)
            for c in range(pair_ref.shape[0]):
                pair_ref[c] = signed[:, c * LANES:(c + 1) * LANES]
                o_ref[0, :, c * LANES:(c + 1) * LANES] = (pair_ref[c, pl.ds(0, group, stride=2), :]
                                                          + pair_ref[c, pl.ds(1, group, stride=2), :])
        else:
            o_ref[0] = o_full


def _decode_attention(kind, q, k_new, v_new, cache_k, cache_v, layer_j, page_table, extras, lam_init=0.0):
    nb, n_rows, _ = q.shape
    vw = v_new.shape[-1]
    n_pages = page_table.shape[1]
    page = cache_k.shape[2] // n_rows
    group = n_rows if kind == "c" else n_rows // 2
    ncols = page * group
    per_step = math.gcd(DECODE_PAGES, n_pages)
    per_b = lambda shape: pl.BlockSpec((1,) + shape, lambda b, p, pt: (b,) + (0,) * len(shape))

    def paged(shape, g):
        return pl.BlockSpec((None, None) + shape,
                            lambda b, p, pt: (layer_j, pt[b, n_pages - 1 - (p * per_step + g)]) + (0,) * len(shape))

    steps = range(per_step)
    in_specs = ([per_b((n_rows, HEAD)), per_b((n_rows, HEAD)), per_b((n_rows, vw))]
                + [paged((page * n_rows, HEAD), g) for g in steps]
                + [paged(cache_v.shape[2:], g) for g in steps])
    args = [q, k_new, v_new] + [cache_k] * per_step + [cache_v] * per_step
    scratch = [pltpu.VMEM((n_rows, HEAD), BF16), pltpu.VMEM((n_rows, ncols), F32),
               pltpu.VMEM((n_rows, 1), F32), pltpu.VMEM((n_rows, 1), F32), pltpu.VMEM((n_rows, vw), F32)]
    if kind == "c":
        cache_logf, logf_new = extras
        in_specs += [paged(cache_logf.shape[2:], g) for g in steps] + [per_b((1, ncols))]
        args += [cache_logf] * per_step + [logf_new]
        scratch += [pltpu.VMEM((1, ncols), F32)]
    else:
        in_specs += [pl.BlockSpec((1, HEAD), lambda b, p, pt: (0, 0))] * 4
        args += [e.reshape(1, HEAD).astype(F32) for e in extras]
        scratch += [pltpu.VMEM((vw // LANES, n_rows, LANES), F32)]
    return pl.pallas_call(
        functools.partial(_decode_kernel, kind=kind, n_pages=n_pages, page=page, n_rows=n_rows,
                          per_step=per_step, lam_init=lam_init),
        grid_spec=pltpu.PrefetchScalarGridSpec(
            num_scalar_prefetch=1,
            grid=(nb, n_pages // per_step),
            in_specs=in_specs,
            out_specs=pl.BlockSpec((1, group, vw), lambda b, p, pt: (b, 0, 0)),
            scratch_shapes=scratch),
        out_shape=jax.ShapeDtypeStruct((nb, group, vw), F32),
        compiler_params=_cp(2),
        name="decode_" + kind,
    )(page_table, *args)


def _layer_a(hp, hs, xp, xs, cache_k, cache_v, j, page_table, w_in, q_norm, k_norm, lams, subln_g, w_out,
             layer_idx, b, t):
    d = xp.shape[1]
    lam_init = 0.8 - 0.6 * math.exp(-0.3 * layer_idx)
    n_layers, n_phys, page, n_rows, _ = cache_k.shape
    ck = cache_k.reshape(n_layers, n_phys, page * n_rows, HEAD)

    nb = hs.shape[0]
    (q,), (qs,) = _matmul(hp, hs, w_in, layer=j, col0=0, ncols=d, mode="gnorm", aux=q_norm, out_dtypes=(BF16,))
    (k32, k16), (ks,) = _matmul(hp, hs, w_in, layer=j, col0=d, ncols=d, mode="gnorm", aux=k_norm,
                                out_dtypes=(F32, BF16))
    (v32, v16), (vs,) = _matmul(hp, hs, w_in, layer=j, col0=2 * d, ncols=d, out_dtypes=(F32, BF16))
    o = _attn_a_prompt(q.reshape(b, t, d), k16.reshape(b, t, d), v16.reshape(b, t, d), *lams, subln_g, lam_init)
    v_rows = jnp.repeat(vs.reshape(nb, n_rows // 2, 2 * HEAD), 2, axis=1)
    os_ = _decode_attention("a", qs.reshape(nb, n_rows, HEAD), ks.reshape(nb, n_rows, HEAD), v_rows,
                            ck, cache_v, j, page_table, lams, lam_init)
    os_ = _rmsnorm(os_.reshape(nb * n_rows // 2, 2 * HEAD), subln_g, F32, post_scale=1.0 - lam_init)
    (xp,), (xs,) = _matmul(o.reshape(b * t, d), os_.reshape(nb, d), w_out, layer=j, mode="res", aux=xp, aux_s=xs)
    return xp, xs, k32, v32, ks, vs


def _layer_b(hp, hs, xp, xs, state, j, w_in, lower_bounds, g_norm, w_out, layer_idx, b, t):
    (proj,), (proj_s,) = _matmul(hp, hs, w_in, layer=j)
    o, s_p = _gla_prompt(proj, lower_bounds, g_norm, layer_idx, b, t)
    o_s, s_s = _gla_step(proj_s, state, j, lower_bounds, g_norm, layer_idx)
    (xp,), (xs,) = _matmul(o, o_s, w_out, layer=j, mode="res", aux=xp, aux_s=xs)
    return xp, xs, s_p, s_s


def _layer_c(hp, hs, xp, xs, cache_k, cache_v, cache_logf, j, page_table, w_in, b_f, q_norm, k_norm, w_out, b, t):
    d = xp.shape[1]
    n_heads = d // HEAD
    n_layers, n_phys, page = cache_k.shape[:3]
    ck = cache_k.reshape(n_layers, n_phys, page * n_heads, HEAD)
    cv = cache_v.reshape(n_layers, n_phys, page * n_heads, HEAD)
    clf = cache_logf.reshape(n_layers, n_phys, page * n_heads // LANES, LANES)
    w_f = jnp.pad(w_in[j, :, 3 * d:], ((0, 0), (0, LANES - n_heads)))
    b_pad = jnp.pad(b_f.astype(F32), (0, LANES - n_heads))

    nb = hs.shape[0]
    (q,), (qs,) = _matmul(hp, hs, w_in, layer=j, col0=0, ncols=d, mode="gnorm", aux=q_norm, out_dtypes=(BF16,))
    (k32, k16), (ks,) = _matmul(hp, hs, w_in, layer=j, col0=d, ncols=d, mode="gnorm", aux=k_norm,
                                out_dtypes=(F32, BF16))
    (v32, v16), (vs,) = _matmul(hp, hs, w_in, layer=j, col0=2 * d, ncols=d, out_dtypes=(F32, BF16))
    (logf,), (lfs_pad,) = _matmul(hp, hs, w_f, mode="logsig", aux=b_pad)
    lfs = lfs_pad[:, :n_heads]
    f_t = _logf_cumsum_t(logf.reshape(b, t, LANES), n_heads)
    o = _attn_c_prompt(q.reshape(b, t, d), k16.reshape(b, t, d), v16.reshape(b, t, d),
                       f_t.reshape(b, n_heads, 1, t))
    lf_new = jnp.tile(lfs, (1, page)).reshape(nb, 1, page * n_heads)
    os_ = _decode_attention("c", qs.reshape(nb, n_heads, HEAD), ks.reshape(nb, n_heads, HEAD),
                            vs.reshape(nb, n_heads, HEAD), ck, cv, j, page_table, (clf, lf_new))
    (xp,), (xs,) = _matmul(o.reshape(b * t, d), os_.reshape(nb, d), w_out, layer=j, mode="res", aux=xp, aux_s=xs)
    return xp, xs, k32, v32, logf[:, :n_heads], ks, vs, lfs


def _ffn(xp, xs, g, w_gu, w_down, layer):
    d_ff = w_down.shape[1]
    hp = _rmsnorm(xp, g, BF16)
    hs = _rmsnorm(xs, g, F32)
    (act,), (act_s,) = _matmul(hp, hs, w_gu, layer=layer, col0=0, ncols=d_ff, col1=d_ff, mode="swiglu",
                               out_dtypes=(BF16,), out_dtypes_s=(BF16,), tn=512)
    (xp,), (xs,) = _matmul(act, act_s, w_down, layer=layer, mode="res", aux=xp, aux_s=xs, tn=512)
    return xp, xs


def kernel(x_prompt, x_sample, cache_k_a, cache_v_a, state_s_b, cache_k_c, cache_v_c, cache_logf_c, page_table, norm1_g, norm2_g, a_w_in, a_q_norm, a_k_norm, a_lam_q1, a_lam_k1, a_lam_q2, a_lam_k2, a_subln_g, a_w_out, b_w_in, b_lower_bounds, b_out_norm_g, b_w_out, c_w_in, c_b_f, c_q_norm, c_k_norm, c_w_out, ffn_w_gu, ffn_w_down):
    b, t, d = x_prompt.shape
    nb, ds, _ = x_sample.shape
    assert ds == 1 and d % (2 * HEAD) == 0 and t % GLA_CHUNK == 0
    depth = norm1_g.shape[0]
    xp = x_prompt.reshape(b * t, d)
    xs = x_sample.reshape(nb, d)
    h_a, h_c = d // (2 * HEAD), d // HEAD
    outs = {name: [] for name in ("ka_p", "va_p", "ka_s", "va_s", "sb_p", "sb_s",
                                  "kc_p", "vc_p", "lc_p", "kc_s", "vc_s", "lc_s")}
    for i in range(depth):
        j = i // 3
        hp = _rmsnorm(xp, norm1_g[i], BF16)
        hs = _rmsnorm(xs, norm1_g[i], F32)
        kind = i % 3
        if kind == 0:
            lams = (a_lam_q1[j], a_lam_k1[j], a_lam_q2[j], a_lam_k2[j])
            xp, xs, k1, v1, k2, v2 = _layer_a(hp, hs, xp, xs, cache_k_a, cache_v_a, j, page_table, a_w_in,
                                              a_q_norm[j], a_k_norm[j], lams, a_subln_g[j], a_w_out, i, b, t)
            outs["ka_p"].append(k1.reshape(b, t, 2 * h_a, HEAD))
            outs["va_p"].append(v1.reshape(b, t, h_a, 2 * HEAD))
            outs["ka_s"].append(k2.reshape(nb, 1, 2 * h_a, HEAD))
            outs["va_s"].append(v2.reshape(nb, 1, h_a, 2 * HEAD))
        elif kind == 1:
            xp, xs, s1, s2 = _layer_b(hp, hs, xp, xs, state_s_b, j, b_w_in, b_lower_bounds,
                                      b_out_norm_g[j], b_w_out, i, b, t)
            outs["sb_p"].append(s1)
            outs["sb_s"].append(s2)
        else:
            xp, xs, k1, v1, l1, k2, v2, l2 = _layer_c(hp, hs, xp, xs, cache_k_c, cache_v_c, cache_logf_c, j,
                                                      page_table, c_w_in, c_b_f[j], c_q_norm[j],
                                                      c_k_norm[j], c_w_out, b, t)
            outs["kc_p"].append(k1.reshape(b, t, h_c, HEAD))
            outs["vc_p"].append(v1.reshape(b, t, h_c, HEAD))
            outs["lc_p"].append(l1.reshape(b, t, h_c))
            outs["kc_s"].append(k2.reshape(nb, 1, h_c, HEAD))
            outs["vc_s"].append(v2.reshape(nb, 1, h_c, HEAD))
            outs["lc_s"].append(l2.reshape(nb, 1, h_c))
        xp, xs = _ffn(xp, xs, norm2_g[i], ffn_w_gu, ffn_w_down, i)
    st = lambda name: jnp.stack(outs[name])
    return (xp.reshape(b, t, d), xs.reshape(nb, 1, d), st("ka_p"), st("va_p"), st("ka_s"), st("va_s"),
            st("sb_p"), st("sb_s"), st("kc_p"), st("vc_p"), st("lc_p"), st("kc_s"), st("vc_s"), st("lc_s"))
```

```python
import functools
import math

import numpy as np
import jax
import jax.numpy as jnp
from jax import lax
from jax.experimental import pallas as pl
from jax.experimental.pallas import tpu as pltpu

F32 = jnp.float32
BF16 = jnp.bfloat16
EPS = 1e-6
LANES = 128
SUBLANES = 8
VMEM_LIMIT = 56 * 1024 * 1024
MM_VMEM_BUDGET = 48 * 1024 * 1024
HEAD = 128
GLA_CHUNK = 128
ATT_TQ_A = 512
ATT_TQ_C = 256
ATT_HEADS_C = 2
ATT_CK_A = 512
ATT_CK_C = 256
DECODE_PAGES = 8
GLA_HEADS = 8
LOG2E = 1.4426950408889634
NEG_INF = float("-inf")


def _cp(n_axes):
    return pltpu.CompilerParams(dimension_semantics=("arbitrary",) * n_axes,
                                vmem_limit_bytes=VMEM_LIMIT)


def _sigmoid(x):
    return 1.0 / (1.0 + jnp.exp(-x))


def _silu(x):
    return x * _sigmoid(x)


def _log_sigmoid(x):
    return jnp.minimum(x, 0.0) - jnp.log1p(jnp.exp(-jnp.abs(x)))


def _nt_dot(a, b):
    return lax.dot_general(a, b, (((1,), (1,)), ((), ())), preferred_element_type=F32)


def _rmsnorm_kernel(x_ref, g_ref, o_ref, *, post_scale):
    x = x_ref[...].astype(F32)
    ms = jnp.mean(x * x, axis=-1, keepdims=True)
    y = x * lax.rsqrt(ms + EPS) * g_ref[...]
    if post_scale != 1.0:
        y = y * post_scale
    o_ref[...] = y.astype(o_ref.dtype)


def _rmsnorm(x, g, out_dtype, post_scale=1.0):
    rows, d = x.shape
    tr = min(rows, 1024)
    return pl.pallas_call(
        functools.partial(_rmsnorm_kernel, post_scale=post_scale),
        grid=(rows // tr,),
        in_specs=[pl.BlockSpec((tr, d), lambda i: (i, 0)),
                  pl.BlockSpec((1, d), lambda i: (0, 0))],
        out_specs=pl.BlockSpec((tr, d), lambda i: (i, 0)),
        out_shape=jax.ShapeDtypeStruct((rows, d), out_dtype),
        compiler_params=_cp(1),
        name="rmsnorm",
    )(x, g.reshape(1, d).astype(F32))


def _mm_epilogue(acc, up, aux, out_refs, mode, tn):
    if mode == "swiglu":
        acc = _silu(acc) * up
    elif mode == "res":
        acc = aux + acc
    elif mode == "logsig":
        acc = _log_sigmoid(acc + aux)
    if mode == "gnorm":
        for c in range(tn // HEAD):
            blk = acc[:, c * HEAD:(c + 1) * HEAD]
            ms = jnp.mean(blk * blk, axis=-1, keepdims=True)
            y = blk * lax.rsqrt(ms + EPS) * aux
            for o_ref in out_refs:
                o_ref[:, c * HEAD:(c + 1) * HEAD] = y.astype(o_ref.dtype)
    else:
        for o_ref in out_refs:
            o_ref[...] = acc.astype(o_ref.dtype)


def _mm_kernel(*refs, mode, n_out, n_out_s, tn):
    dual = mode == "swiglu"
    has_aux = mode in ("gnorm", "res", "logsig")
    it = iter(refs)
    x_ref = next(it)
    w_ref = next(it)
    w2_ref = next(it) if dual else None
    aux_ref = next(it) if has_aux else None
    xs_ref = next(it)
    aux_s_ref = next(it) if mode == "res" else aux_ref
    out_refs = [next(it) for _ in range(n_out)]
    out_s_refs = [next(it) for _ in range(n_out_s)]
    wbf_ref = next(it)
    wbf2_ref = next(it) if dual else None

    @pl.when(pl.program_id(1) == 0)
    def _():
        wbf_ref[...] = w_ref[...].astype(BF16)
        if dual:
            wbf2_ref[...] = w2_ref[...].astype(BF16)
        xs = xs_ref[...].astype(BF16)
        acc_s = jnp.dot(xs, wbf_ref[...], preferred_element_type=F32)
        up_s = jnp.dot(xs, wbf2_ref[...], preferred_element_type=F32) if dual else None
        _mm_epilogue(acc_s, up_s, aux_s_ref[...] if has_aux else None, out_s_refs, mode, tn)

    x = x_ref[...].astype(BF16)
    acc = jnp.dot(x, wbf_ref[...], preferred_element_type=F32)
    up = jnp.dot(x, wbf2_ref[...], preferred_element_type=F32) if dual else None
    _mm_epilogue(acc, up, aux_ref[...] if has_aux else None, out_refs, mode, tn)


def _matmul(x, xs, w, *, layer=0, col0=0, ncols=None, mode="plain", out_dtypes=(F32,), out_dtypes_s=(F32,),
            aux=None, aux_s=None, col1=None, tn=1024):
    if w.ndim == 2:
        w = w[None]
    m, k = x.shape
    ms = xs.shape[0]
    ncols = w.shape[2] - col0 if ncols is None else ncols
    tn = min(tn, ncols)
    n_w = 2 if mode == "swiglu" else 1

    def vmem_estimate(rows):
        blocks = rows * k * x.dtype.itemsize + n_w * k * tn * 4
        blocks += sum(rows * tn * jnp.dtype(dt).itemsize for dt in out_dtypes)
        blocks += rows * tn * 4 if mode == "res" else 0
        return 2 * blocks + n_w * k * tn * 2

    tm = next((r for r in (1024, 512, 256) if m % r == 0 and vmem_estimate(r) <= MM_VMEM_BUDGET), min(m, 128))
    assert m % tm == 0 and ncols % tn == 0 and col0 % tn == 0
    nb0 = col0 // tn
    in_specs = [pl.BlockSpec((tm, k), lambda n, i: (i, 0)),
                pl.BlockSpec((None, k, tn), lambda n, i: (layer, 0, n + nb0))]
    args = [x, w]
    scratch = [pltpu.VMEM((k, tn), BF16)]
    if mode == "swiglu":
        assert col1 % tn == 0
        nb1 = col1 // tn
        in_specs.append(pl.BlockSpec((None, k, tn), lambda n, i: (layer, 0, n + nb1)))
        args.append(w)
        scratch.append(pltpu.VMEM((k, tn), BF16))
    if mode == "gnorm":
        in_specs.append(pl.BlockSpec((1, HEAD), lambda n, i: (0, 0)))
        args.append(aux.reshape(1, HEAD).astype(F32))
    elif mode == "res":
        in_specs.append(pl.BlockSpec((tm, tn), lambda n, i: (i, n)))
        args.append(aux)
    elif mode == "logsig":
        in_specs.append(pl.BlockSpec((1, tn), lambda n, i: (0, n)))
        args.append(aux.reshape(1, ncols).astype(F32))
    in_specs.append(pl.BlockSpec((ms, k), lambda n, i: (0, 0)))
    args.append(xs)
    if mode == "res":
        in_specs.append(pl.BlockSpec((ms, tn), lambda n, i: (0, n)))
        args.append(aux_s)
    outs = pl.pallas_call(
        functools.partial(_mm_kernel, mode=mode, n_out=len(out_dtypes), n_out_s=len(out_dtypes_s), tn=tn),
        grid=(ncols // tn, m // tm),
        in_specs=in_specs,
        out_specs=([pl.BlockSpec((tm, tn), lambda n, i: (i, n)) for _ in out_dtypes]
                   + [pl.BlockSpec((ms, tn), lambda n, i: (0, n)) for _ in out_dtypes_s]),
        out_shape=([jax.ShapeDtypeStruct((m, ncols), dt) for dt in out_dtypes]
                   + [jax.ShapeDtypeStruct((ms, ncols), dt) for dt in out_dtypes_s]),
        scratch_shapes=scratch,
        compiler_params=_cp(2),
        name="mm_" + mode,
    )(*args)
    return tuple(outs[:len(out_dtypes)]), tuple(outs[len(out_dtypes):])


def _lam_value(lq1, lk1, lq2, lk2, lam_init):
    a = jnp.sum(lq1[...] * lk1[...], axis=-1, keepdims=True)
    b = jnp.sum(lq2[...] * lk2[...], axis=-1, keepdims=True)
    return jnp.exp(a) - jnp.exp(b) + lam_init


def _causal_rows(qs, k_ref, v_ref, s_ref, bias_of_chunk, qi, n_chunks, tq, ck, value_cols):
    cs = (HEAD ** -0.5) * LOG2E
    n_maps = len(qs)
    row = lax.broadcasted_iota(jnp.int32, (tq, ck), 0)
    col = lax.broadcasted_iota(jnp.int32, (tq, ck), 1)
    m_part = [jnp.full((tq, LANES), NEG_INF, F32) for _ in range(n_maps)]
    for c in range(n_chunks):
        k = k_ref[0, c * ck:(c + 1) * ck, :]
        for j in range(n_maps):
            s = _nt_dot(qs[j], k[:, j * HEAD:(j + 1) * HEAD]) * cs + bias_of_chunk(c, j)
            if c == n_chunks - 1:
                s = jnp.where(row - col >= c * ck - qi * tq, s, NEG_INF)
            s_ref[j, :, c * ck:(c + 1) * ck] = s
            for g in range(ck // LANES):
                m_part[j] = jnp.maximum(m_part[j], s[:, g * LANES:(g + 1) * LANES])

    outs = []
    for j in range(n_maps):
        m = jnp.max(m_part[j], axis=1, keepdims=True)
        l_part = jnp.zeros((tq, LANES), F32)
        o = None
        for c in range(n_chunks):
            p = jnp.exp2(s_ref[j, :, c * ck:(c + 1) * ck] - m)
            for g in range(ck // LANES):
                l_part = l_part + p[:, g * LANES:(g + 1) * LANES]
            pv = jnp.dot(p.astype(BF16), v_ref[0, c * ck:(c + 1) * ck, value_cols[j]],
                         preferred_element_type=F32)
            o = pv if o is None else o + pv
        outs.append(o * (1.0 / jnp.sum(l_part, axis=1, keepdims=True)))
    return outs


def _for_each_chunk_count(qi, tq, ck, t, fn):
    per = ck // tq
    for n_chunks in range(1, t // ck + 1):
        pl.when(qi // per == n_chunks - 1)(functools.partial(fn, n_chunks))


def _attn_a_kernel(q_ref, k_ref, v_ref, lq1, lk1, lq2, lk2, sg_ref, o_ref, s_ref,
                   *, tq, ck, t, n_heads, lam_init):
    h = pl.program_id(1)
    qi = pl.program_id(2)
    slope2 = LOG2E * jnp.exp2(-8.0 * (jnp.full((1, 1), h, jnp.int32).astype(F32) + 1.0) / n_heads)
    lam = _lam_value(lq1, lk1, lq2, lk2, lam_init)
    col = lax.broadcasted_iota(jnp.int32, (1, ck), 1)
    q = q_ref[0]
    qs = [q[:, j * HEAD:(j + 1) * HEAD] for j in range(2)]

    def run(n_chunks):
        bias = lambda c, j: slope2 * (col + (c * ck - qi * tq)).astype(F32)
        both = slice(0, 2 * HEAD)
        o1, o2 = _causal_rows(qs, k_ref, v_ref, s_ref, bias, qi, n_chunks, tq, ck, (both, both))
        o = o1 - lam * o2
        ms =jnp.mean(o * o, axis=-1, keepdims=True)
        y = o * lax.rsqrt(ms + EPS) * sg_ref[...]
        o_ref[0] = (y * (1.0 - lam_init)).astype(o_ref.dtype)

    _for_each_chunk_count(qi, tq, ck, t, run)


def _attn_tiles(t, tq, ck):
    tq = min(tq, t)
    ck = min(ck, t)
    assert t % ck == 0 and ck % tq == 0
    return tq, ck


def _attn_a_prompt(q, k, v, lq1, lk1, lq2, lk2, subln_g, lam_init):
    b, t, d = q.shape
    n_heads = d // (2 * HEAD)
    tq, ck = _attn_tiles(t, ATT_TQ_A, ATT_CK_A)
    vec = lambda a: a.reshape(1, -1).astype(F32)
    small = lambda w: pl.BlockSpec((1, w), lambda bi, h, qi: (0, 0))
    return pl.pallas_call(
        functools.partial(_attn_a_kernel, tq=tq, ck=ck, t=t, n_heads=n_heads, lam_init=lam_init),
        grid=(b, n_heads, t // tq),
        in_specs=[pl.BlockSpec((1, tq, 2 * HEAD), lambda bi, h, qi: (bi, qi, h)),
                  pl.BlockSpec((1, t, 2 * HEAD), lambda bi, h, qi: (bi, 0, h)),
                  pl.BlockSpec((1, t, 2 * HEAD), lambda bi, h, qi: (bi, 0, h)),
                  small(HEAD), small(HEAD), small(HEAD), small(HEAD), small(2 * HEAD)],
        out_specs=pl.BlockSpec((1, tq, 2 * HEAD), lambda bi, h, qi: (bi, qi, h)),
        out_shape=jax.ShapeDtypeStruct((b, t, d), BF16),
        scratch_shapes=[pltpu.VMEM((2, tq, t), F32)],
        compiler_params=_cp(3),
        name="attn_a_prompt",
    )(q, k, v, vec(lq1), vec(lk1), vec(lq2), vec(lk2), vec(subln_g))


def _attn_c_kernel(q_ref, k_ref, v_ref, f_ref, o_ref, s_ref, *, tq, ck, t, hg):
    qi = pl.program_id(2)
    q_start = pl.multiple_of(qi * tq, tq)
    f_first = [f_ref[0, j, :, pl.ds(q_start, tq)][:, 0:1] for j in range(hg)]
    cols = [slice(j * HEAD, (j + 1) * HEAD) for j in range(hg)]
    q = q_ref[0]

    def run(n_chunks):
        bias = lambda c, j: (f_first[j] - f_ref[0, j, :, c * ck:(c + 1) * ck]) * LOG2E
        outs = _causal_rows([q[:, cj] for cj in cols], k_ref, v_ref, s_ref, bias, qi, n_chunks, tq, ck, cols)
        for cj, o in zip(cols, outs):
            o_ref[0, :, cj] = o.astype(o_ref.dtype)

    _for_each_chunk_count(qi, tq, ck, t, run)


def _attn_c_prompt(q, k, v, f_t):
    b, t, d = q.shape
    n_heads = d // HEAD
    hg = math.gcd(ATT_HEADS_C, n_heads)
    tq, ck = _attn_tiles(t, ATT_TQ_C, ATT_CK_C)
    return pl.pallas_call(
        functools.partial(_attn_c_kernel, tq=tq, ck=ck, t=t, hg=hg),
        grid=(b, n_heads // hg, t // tq),
        in_specs=[pl.BlockSpec((1, tq, hg * HEAD), lambda bi, h, qi: (bi, qi, h)),
                  pl.BlockSpec((1, t, hg * HEAD), lambda bi, h, qi: (bi, 0, h)),
                  pl.BlockSpec((1, t, hg * HEAD), lambda bi, h, qi: (bi, 0, h)),
                  pl.BlockSpec((1, hg, 1, t), lambda bi, h, qi: (bi, h, 0, 0))],
        out_specs=pl.BlockSpec((1, tq, hg * HEAD), lambda bi, h, qi: (bi, qi, h)),
        out_shape=jax.ShapeDtypeStruct((b, t, d), BF16),
        scratch_shapes=[pltpu.VMEM((hg, tq, t), F32)],
        compiler_params=_cp(3),
        name="attn_c_prompt",
    )(q, k, v, f_t)


def _cumsum_rows(x, n):
    row = lax.broadcasted_iota(jnp.int32, x.shape, 0)
    d = 1
    while d < n:
        x = x + jnp.where(row >= d, pltpu.roll(x, d, 0), 0.0)
        d *= 2
    return x


def _logf_cumsum_kernel(x_ref, o_ref, carry_ref, *, tc, n_heads):
    @pl.when(pl.program_id(1) == 0)
    def _():
        carry_ref[...] = jnp.zeros(carry_ref.shape, F32)

    f = _cumsum_rows(x_ref[0], tc) + carry_ref[...]
    carry_ref[...] = f[tc - 1:tc, :]
    o_ref[0] = f.T[0:n_heads, :]


def _logf_cumsum_t(logf_pad, n_heads, tc=512):
    b, t, _ = logf_pad.shape
    tc = min(tc, t)
    return pl.pallas_call(
        functools.partial(_logf_cumsum_kernel, tc=tc, n_heads=n_heads),
        grid=(b, t // tc),
        in_specs=[pl.BlockSpec((1, tc, LANES), lambda bi, c: (bi, c, 0))],
        out_specs=pl.BlockSpec((1, n_heads, tc), lambda bi, c: (bi, 0, c)),
        out_shape=jax.ShapeDtypeStruct((b, n_heads, t), F32),
        scratch_shapes=[pltpu.VMEM((1, LANES), F32)],
        compiler_params=_cp(2),
        name="logf_cumsum",
    )(logf_pad)


def _gla_levels():
    return [GLA_CHUNK >> (i + 1) for i in range(int(math.log2(GLA_CHUNK)))]


def _gla_masks():
    t = np.arange(GLA_CHUNK)[:, None]
    s = np.arange(GLA_CHUNK)[None, :]
    masks = []
    for b in _gla_levels():
        masks.append((t // (2 * b) == s // (2 * b)) & (t % (2 * b) >= b) & (s % (2 * b) < b))
    masks.append(t == s)
    return jnp.asarray(np.stack(masks).astype(np.float32))


def _level_reference(g_ref, hh, b):
    row = lambda r, n: jnp.broadcast_to(g_ref[hh, r:r + 1, :], (n, LANES))
    if 2 * b >= SUBLANES:
        return jnp.concatenate([row(2 * b * j + b - 1, 2 * b) for j in range(GLA_CHUNK // (2 * b))], axis=0)
    sub = lax.broadcasted_iota(jnp.int32, (SUBLANES, LANES), 0)
    pieces = []
    for j in range(GLA_CHUNK // SUBLANES):
        base = SUBLANES * j
        n_blk = SUBLANES // (2 * b)
        piece = row(base + (n_blk - 1) * 2 * b + b - 1, SUBLANES)
        for i in range(n_blk - 2, -1, -1):
            piece = jnp.where(sub < (i + 1) * 2 * b, row(base + i * 2 * b + b - 1, SUBLANES), piece)
        pieces.append(piece)
    return jnp.concatenate(pieces, axis=0)


def _lower_bound(lb_ref, layer_idx):
    x = lb_ref[...].astype(F32)
    e = jnp.exp(x - jnp.max(x, axis=0, keepdims=True))
    p = e / jnp.sum(e, axis=0, keepdims=True)
    r = lax.broadcasted_iota(jnp.int32, p.shape, 0)
    return jnp.sum(jnp.where((r >= 1) & (r <= layer_idx), p, 0.0), axis=0, keepdims=True)


def _gla_kernel(q_ref, f_ref, i_ref, g_ref, lb_ref, gn_ref, mask_ref, o_ref, s_ref,
                st_ref, gs_ref, *, layer_idx, n_chunks, hg):
    c = pl.program_id(2)
    scale = HEAD ** -0.5

    @pl.when(c == 0)
    def _():
        st_ref[...] = jnp.zeros(st_ref.shape, F32)

    lb_all = _lower_bound(lb_ref, layer_idx)
    levels = _gla_levels()
    for hh in range(hg):
        cols = slice(hh * HEAD, (hh + 1) * HEAD)
        lb = lb_all[:, cols]
        q = _silu(q_ref[:, cols]) * scale
        gate = lb + (1.0 - lb) * _sigmoid(f_ref[:, cols])
        kk = 1.0 - gate
        v = i_ref[:, cols]
        v_bf = v.astype(BF16)
        g_cum = _cumsum_rows(jnp.log(gate), GLA_CHUNK)
        gs_ref[hh] = g_cum
        st = st_ref[hh]
        o = _nt_dot((q * jnp.exp(g_cum)).astype(BF16), st.astype(BF16))

        a = jnp.sum(q * kk, axis=1, keepdims=True) * mask_ref[len(levels)]
        for lvl, b in enumerate(levels):
            e = jnp.exp(-jnp.abs(g_cum - _level_reference(gs_ref, hh, b)))
            qs = (q * e).astype(BF16)
            ks = (kk * e).astype(BF16)
            a = a + _nt_dot(qs, ks) * mask_ref[lvl]
        o = o + jnp.dot(a.astype(BF16), v_bf, preferred_element_type=F32)

        g_last = gs_ref[hh, GLA_CHUNK - 1:GLA_CHUNK, :]
        k_dec = (kk * jnp.exp(g_last - g_cum)).astype(BF16)
        st_new = st * jnp.exp(g_last) + jnp.dot(v.T.astype(BF16), k_dec, preferred_element_type=F32)
        st_ref[hh] = st_new

        ms = jnp.mean(o * o, axis=-1, keepdims=True)
        y = o * lax.rsqrt(ms + EPS) * gn_ref[...]
        o_ref[:, cols] = (y * _silu(g_ref[:, cols])).astype(o_ref.dtype)

    @pl.when(c == n_chunks - 1)
    def _():
        for hh in range(hg):
            s_ref[hh] = st_ref[hh].T


def _gla_prompt(proj, lower_bounds, g_norm, layer_idx, b, t):
    m, d4 = proj.shape
    d = d4 // 4
    n_heads = d // HEAD
    hg = math.gcd(GLA_HEADS, n_heads)
    n_groups = n_heads // hg
    depth = lower_bounds.shape[0]
    n_chunks = t // GLA_CHUNK
    masks = _gla_masks()
    blk = lambda off: pl.BlockSpec((GLA_CHUNK, hg * HEAD),
                                   lambda bi, h, c: (bi * n_chunks + c, off * n_groups + h))
    return pl.pallas_call(
        functools.partial(_gla_kernel, layer_idx=layer_idx, n_chunks=n_chunks, hg=hg),
        grid=(b, n_groups, n_chunks),
        in_specs=[blk(0), blk(1), blk(2), blk(3),
                  pl.BlockSpec((depth, hg * HEAD), lambda bi, h, c: (0, h)),
                  pl.BlockSpec((1, HEAD), lambda bi, h, c: (0, 0)),
                  pl.BlockSpec(masks.shape, lambda bi, h, c: (0, 0, 0))],
        out_specs=[pl.BlockSpec((GLA_CHUNK, hg * HEAD), lambda bi, h, c: (bi * n_chunks + c, h)),
                   pl.BlockSpec((None, hg, HEAD, HEAD), lambda bi, h, c: (bi, h, 0, 0))],
        out_shape=[jax.ShapeDtypeStruct((m, d), BF16),
                   jax.ShapeDtypeStruct((b, n_heads, HEAD, HEAD), F32)],
        scratch_shapes=[pltpu.VMEM((hg, HEAD, HEAD), F32), pltpu.VMEM((hg, GLA_CHUNK, HEAD), F32)],
        compiler_params=_cp(3),
        name="gla_prompt",
    )(proj, proj, proj, proj, lower_bounds, g_norm.reshape(1, HEAD).astype(F32), masks)


def _gla_step_kernel(i_ref, g_ref, qt_ref, ft_ref, lbt_ref, gn_ref, s_in_ref, o_ref, s_out_ref,
                     *, layer_idx, n_batch):
    scale = HEAD ** -0.5
    xt = lbt_ref[...].astype(F32)
    et = jnp.exp(xt - jnp.max(xt, axis=1, keepdims=True))
    pt = et / jnp.sum(et, axis=1, keepdims=True)
    ct = lax.broadcasted_iota(jnp.int32, pt.shape, 1)
    lb_col = jnp.sum(jnp.where((ct >= 1) & (ct <= layer_idx), pt, 0.0), axis=1, keepdims=True)

    q_col = _silu(qt_ref[...]) * scale
    gate_col = lb_col + (1.0 - lb_col) * _sigmoid(ft_ref[...])
    k_col = 1.0 - gate_col
    a_all = jnp.sum(q_col * k_col, axis=0, keepdims=True)
    qg_col = q_col * gate_col
    for bi in range(n_batch):
        s_old = s_in_ref[bi]
        v_row = i_ref[bi:bi + 1, :]
        o = (jnp.sum(qg_col[:, bi:bi + 1] * s_old, axis=0, keepdims=True)
             + a_all[:, bi:bi + 1] * v_row)
        s_out_ref[bi] = gate_col[:, bi:bi + 1] * s_old + k_col[:, bi:bi + 1] * v_row
        ms = jnp.mean(o * o, axis=-1, keepdims=True)
        y = o * lax.rsqrt(ms + EPS) * gn_ref[...]
        o_ref[bi:bi + 1, :] = y * _silu(g_ref[bi:bi + 1, :])


def _gla_step(proj, state, layer_j, lower_bounds, g_norm, layer_idx):
    nb, d4 = proj.shape
    d = d4 // 4
    n_heads = d // HEAD
    depth = lower_bounds.shape[0]
    proj_t = proj.T
    lb_t = lower_bounds.T
    row = lambda off: pl.BlockSpec((nb, HEAD), lambda h: (0, off * n_heads + h))
    col = lambda off: pl.BlockSpec((HEAD, nb), lambda h: (off * n_heads + h, 0))
    return pl.pallas_call(
        functools.partial(_gla_step_kernel, layer_idx=layer_idx, n_batch=nb),
        grid=(n_heads,),
        in_specs=[row(2), row(3), col(0), col(1),
                  pl.BlockSpec((HEAD, depth), lambda h: (h, 0)),
                  pl.BlockSpec((1, HEAD), lambda h: (0, 0)),
                  pl.BlockSpec((None, nb, None, HEAD, HEAD), lambda h: (layer_j, 0, h, 0, 0))],
        out_specs=[pl.BlockSpec((nb, HEAD), lambda h: (0, h)),
                   pl.BlockSpec((nb, None, HEAD, HEAD), lambda h: (0, h, 0, 0))],
        out_shape=[jax.ShapeDtypeStruct((nb, d), F32),
                   jax.ShapeDtypeStruct((nb, n_heads, HEAD, HEAD), F32)],
        compiler_params=_cp(1),
        name="gla_step",
    )(proj, proj, proj_t, proj_t, lb_t, g_norm.reshape(1, HEAD).astype(F32), state)


def _decode_kernel(pt_ref, *refs, kind, n_pages, page, n_rows, per_step, lam_init):
    it = iter(refs)
    q_ref = next(it)
    kn_ref = next(it)
    vn_ref = next(it)
    kc_refs = [next(it) for _ in range(per_step)]
    vc_refs = [next(it) for _ in range(per_step)]
    if kind == "c":
        lfc_refs = [next(it) for _ in range(per_step)]
        lfn_ref = next(it)
    else:
        lq1, lk1, lq2, lk2 = next(it), next(it), next(it), next(it)
    o_ref = next(it)
    qbf_ref, mask_ref, m_ref, l_ref, acc_ref = next(it), next(it), next(it), next(it), next(it)
    if kind == "c":
        carry_ref = next(it)
    else:
        pair_ref = next(it)

    p = pl.program_id(1)
    scale = HEAD ** -0.5
    group = n_rows if kind == "c" else n_rows // 2
    ncols = page * group
    rows_col = lax.broadcasted_iota(jnp.int32, (n_rows, 1), 0)

    @pl.when(p == 0)
    def _():
        qbf_ref[...] = q_ref[0].astype(BF16)
        r = lax.broadcasted_iota(jnp.int32, (n_rows, ncols), 0)
        cidx = lax.broadcasted_iota(jnp.int32, (n_rows, ncols), 1)
        head_of_row = r if kind == "c" else r // 2
        mask_ref[...] = jnp.where(cidx % group == head_of_row, 0.0, NEG_INF)
        m_ref[...] = jnp.full(m_ref.shape, NEG_INF, F32)
        l_ref[...] = jnp.zeros(l_ref.shape, F32)
        acc_ref[...] = jnp.zeros(acc_ref.shape, F32)
        if kind == "c":
            carry_ref[...] = lfn_ref[0]

    lane = lax.broadcasted_iota(jnp.int32, (1, ncols), 1)
    qbf = qbf_ref[...]
    mask = mask_ref[...]
    if kind == "a":
        slope = jnp.exp2(-8.0 * ((rows_col // 2).astype(F32) + 1.0) / group)
    else:
        carry = carry_ref[...]
    scores, values = [], []
    for g in range(per_step):
        page_idx = n_pages - 1 - (p * per_step + g)
        kc_ref, vc_ref = kc_refs[g], vc_refs[g]
        if kind == "a":
            s_even = _nt_dot(qbf, kc_ref[pl.ds(0, ncols, stride=2), :].astype(BF16))
            s_odd = _nt_dot(qbf, kc_ref[pl.ds(1, ncols, stride=2), :].astype(BF16))
            s = jnp.where(rows_col % 2 == 0, s_even, s_odd) * scale
            dist = (n_pages * page - (page_idx * page + lane // group)).astype(F32)
            s = s - slope * dist + mask
            values.append(vc_ref[...].reshape(ncols, vc_ref.shape[-1]))
        else:
            s = _nt_dot(qbf, kc_ref[...].astype(BF16)) * scale
            lf2d = lfc_refs[g][...]
            flat = jnp.concatenate([lf2d[i:i + 1, :] for i in range(lf2d.shape[0])], axis=1)
            y = flat
            tot = flat
            dd = group
            while dd < ncols:
                y = y + jnp.where(lane + dd < ncols, pltpu.roll(y, ncols - dd, 1), 0.0)
                tot = tot + pltpu.roll(tot, dd, 1)
                dd *= 2
            s = s + (y - flat + carry) + mask
            carry = carry + tot
            values.append(vc_ref[...])
        scores.append(s)
    if kind == "c":
        carry_ref[...] = carry

    m_prev = m_ref[...]
    m_new = m_prev
    for s in scores:
        m_new = jnp.maximum(m_new, jnp.max(s, axis=1, keepdims=True))
    alpha = jnp.exp(m_prev - m_new)
    l_new = alpha * l_ref[...]
    acc = alpha * acc_ref[...]
    for s, v2d in zip(scores, values):
        pr = jnp.exp(s - m_new)
        l_new = l_new + jnp.sum(pr, axis=1, keepdims=True)
        acc = acc + jnp.dot(pr.astype(BF16), v2d.astype(BF16), preferred_element_type=F32)
    l_ref[...] = l_new
    acc_ref[...] = acc
    m_ref[...] = m_new

    @pl.when(p == n_pages // per_step - 1)
    def _():
        s_new = jnp.sum(q_ref[0] * kn_ref[0], axis=1, keepdims=True) * scale
        m_prev = m_ref[...]
        m_fin = jnp.maximum(m_prev, s_new)
        alpha = jnp.exp(m_prev - m_fin)
        p_new = jnp.exp(s_new - m_fin)
        l_fin = alpha * l_ref[...] + p_new
        acc = alpha * acc_ref[...] + p_new * vn_ref[0]
        o_full = acc * (1.0 / l_fin)
        if kind == "a":
            lam = _lam_value(lq1, lk1, lq2, lk2, lam_init)
            signed = o_full * jnp.where(rows_col % 2 == 0, 1.0, -lam)
            for c in range(pair_ref.shape[0]):
                pair_ref[c] = signed[:, c * LANES:(c + 1) * LANES]
                o_ref[0, :, c * LANES:(c + 1) * LANES] = (pair_ref[c, pl.ds(0, group, stride=2), :]
                                                          + pair_ref[c, pl.ds(1, group, stride=2), :])
        else:
            o_ref[0] = o_full


def _decode_attention(kind, q, k_new, v_new, cache_k, cache_v, layer_j, page_table, extras, lam_init=0.0):
    nb, n_rows, _ = q.shape
    vw = v_new.shape[-1]
    n_pages = page_table.shape[1]
    page = cache_k.shape[2] // n_rows
    group = n_rows if kind == "c" else n_rows // 2
    ncols = page * group
    per_step = math.gcd(DECODE_PAGES, n_pages)
    per_b = lambda shape: pl.BlockSpec((1,) + shape, lambda b, p, pt: (b,) + (0,) * len(shape))

    def paged(shape, g):
        return pl.BlockSpec((None, None) + shape,
                            lambda b, p, pt: (layer_j, pt[b, n_pages - 1 - (p * per_step + g)]) + (0,) * len(shape))

    steps = range(per_step)
    in_specs = ([per_b((n_rows, HEAD)), per_b((n_rows, HEAD)), per_b((n_rows, vw))]
                + [paged((page * n_rows, HEAD), g) for g in steps]
                + [paged(cache_v.shape[2:], g) for g in steps])
    args = [q, k_new, v_new] + [cache_k] * per_step + [cache_v] * per_step
    scratch = [pltpu.VMEM((n_rows, HEAD), BF16), pltpu.VMEM((n_rows, ncols), F32),
               pltpu.VMEM((n_rows, 1), F32), pltpu.VMEM((n_rows, 1), F32), pltpu.VMEM((n_rows, vw), F32)]
    if kind == "c":
        cache_logf, logf_new = extras
        in_specs += [paged(cache_logf.shape[2:], g) for g in steps] + [per_b((1, ncols))]
        args += [cache_logf] * per_step + [logf_new]
        scratch += [pltpu.VMEM((1, ncols), F32)]
    else:
        in_specs += [pl.BlockSpec((1, HEAD), lambda b, p, pt: (0, 0))] * 4
        args += [e.reshape(1, HEAD).astype(F32) for e in extras]
        scratch += [pltpu.VMEM((vw // LANES, n_rows, LANES), F32)]
    return pl.pallas_call(
        functools.partial(_decode_kernel, kind=kind, n_pages=n_pages, page=page, n_rows=n_rows,
                          per_step=per_step, lam_init=lam_init),
        grid_spec=pltpu.PrefetchScalarGridSpec(
            num_scalar_prefetch=1,
            grid=(nb, n_pages // per_step),
            in_specs=in_specs,
            out_specs=pl.BlockSpec((1, group, vw), lambda b, p, pt: (b, 0, 0)),
            scratch_shapes=scratch),
        out_shape=jax.ShapeDtypeStruct((nb, group, vw), F32),
        compiler_params=_cp(2),
        name="decode_" + kind,
    )(page_table, *args)


def _layer_a(hp, hs, xp, xs, cache_k, cache_v, j, page_table, w_in, q_norm, k_norm, lams, subln_g, w_out,
             layer_idx, b, t):
    d = xp.shape[1]
    lam_init = 0.8 - 0.6 * math.exp(-0.3 * layer_idx)
    n_layers, n_phys, page, n_rows, _ = cache_k.shape
    ck = cache_k.reshape(n_layers, n_phys, page * n_rows, HEAD)

    nb = hs.shape[0]
    (q,), (qs,) = _matmul(hp, hs, w_in, layer=j, col0=0, ncols=d, mode="gnorm", aux=q_norm, out_dtypes=(BF16,))
    (k32, k16), (ks,) = _matmul(hp, hs, w_in, layer=j, col0=d, ncols=d, mode="gnorm", aux=k_norm,
                                out_dtypes=(F32, BF16))
    (v32, v16), (vs,) = _matmul(hp, hs, w_in, layer=j, col0=2 * d, ncols=d, out_dtypes=(F32, BF16))
    o = _attn_a_prompt(q.reshape(b, t, d), k16.reshape(b, t, d), v16.reshape(b, t, d), *lams, subln_g, lam_init)
    v_rows = jnp.repeat(vs.reshape(nb, n_rows // 2, 2 * HEAD), 2, axis=1)
    os_ = _decode_attention("a", qs.reshape(nb, n_rows, HEAD), ks.reshape(nb, n_rows, HEAD), v_rows,
                            ck, cache_v, j, page_table, lams, lam_init)
    os_ = _rmsnorm(os_.reshape(nb * n_rows // 2, 2 * HEAD), subln_g, F32, post_scale=1.0 - lam_init)
    (xp,), (xs,) = _matmul(o.reshape(b * t, d), os_.reshape(nb, d), w_out, layer=j, mode="res", aux=xp, aux_s=xs)
    return xp, xs, k32, v32, ks, vs


def _layer_b(hp, hs, xp, xs, state, j, w_in, lower_bounds, g_norm, w_out, layer_idx, b, t):
    (proj,), (proj_s,) = _matmul(hp, hs, w_in, layer=j)
    o, s_p = _gla_prompt(proj, lower_bounds, g_norm, layer_idx, b, t)
    o_s, s_s = _gla_step(proj_s, state, j, lower_bounds, g_norm, layer_idx)
    (xp,), (xs,) = _matmul(o, o_s, w_out, layer=j, mode="res", aux=xp, aux_s=xs)
    return xp, xs, s_p, s_s


def _layer_c(hp, hs, xp, xs, cache_k, cache_v, cache_logf, j, page_table, w_in, b_f, q_norm, k_norm, w_out, b, t):
    d = xp.shape[1]
    n_heads = d // HEAD
    n_layers, n_phys, page = cache_k.shape[:3]
    ck = cache_k.reshape(n_layers, n_phys, page * n_heads, HEAD)
    cv = cache_v.reshape(n_layers, n_phys, page * n_heads, HEAD)
    clf = cache_logf.reshape(n_layers, n_phys, page * n_heads // LANES, LANES)
    w_f = jnp.pad(w_in[j, :, 3 * d:], ((0, 0), (0, LANES - n_heads)))
    b_pad = jnp.pad(b_f.astype(F32), (0, LANES - n_heads))

    nb = hs.shape[0]
    (q,), (qs,) = _matmul(hp, hs, w_in, layer=j, col0=0, ncols=d, mode="gnorm", aux=q_norm, out_dtypes=(BF16,))
    (k32, k16), (ks,) = _matmul(hp, hs, w_in, layer=j, col0=d, ncols=d, mode="gnorm", aux=k_norm,
                                out_dtypes=(F32, BF16))
    (v32, v16), (vs,) = _matmul(hp, hs, w_in, layer=j, col0=2 * d, ncols=d, out_dtypes=(F32, BF16))
    (logf,), (lfs_pad,) = _matmul(hp, hs, w_f, mode="logsig", aux=b_pad)
    lfs = lfs_pad[:, :n_heads]
    f_t = _logf_cumsum_t(logf.reshape(b, t, LANES), n_heads)
    o = _attn_c_prompt(q.reshape(b, t, d), k16.reshape(b, t, d), v16.reshape(b, t, d),
                       f_t.reshape(b, n_heads, 1, t))
    lf_new = jnp.tile(lfs, (1, page)).reshape(nb, 1, page * n_heads)
    os_ = _decode_attention("c", qs.reshape(nb, n_heads, HEAD), ks.reshape(nb, n_heads, HEAD),
                            vs.reshape(nb, n_heads, HEAD), ck, cv, j, page_table, (clf, lf_new))
    (xp,), (xs,) = _matmul(o.reshape(b * t, d), os_.reshape(nb, d), w_out, layer=j, mode="res", aux=xp, aux_s=xs)
    return xp, xs, k32, v32, logf[:, :n_heads], ks, vs, lfs


def _ffn(xp, xs, g, w_gu, w_down, layer):
    d_ff = w_down.shape[1]
    hp = _rmsnorm(xp, g, BF16)
    hs = _rmsnorm(xs, g, F32)
    (act,), (act_s,) = _matmul(hp, hs, w_gu, layer=layer, col0=0, ncols=d_ff, col1=d_ff, mode="swiglu",
                               out_dtypes=(BF16,), out_dtypes_s=(BF16,), tn=512)
    (xp,), (xs,) = _matmul(act, act_s, w_down, layer=layer, mode="res", aux=xp, aux_s=xs, tn=512)
    return xp, xs


def kernel(x_prompt, x_sample, cache_k_a, cache_v_a, state_s_b, cache_k_c, cache_v_c, cache_logf_c, page_table, norm1_g, norm2_g, a_w_in, a_q_norm, a_k_norm, a_lam_q1, a_lam_k1, a_lam_q2, a_lam_k2, a_subln_g, a_w_out, b_w_in, b_lower_bounds, b_out_norm_g, b_w_out, c_w_in, c_b_f, c_q_norm, c_k_norm, c_w_out, ffn_w_gu, ffn_w_down):
    b, t, d = x_prompt.shape
    nb, ds, _ = x_sample.shape
    assert ds == 1 and d % (2 * HEAD) == 0 and t % GLA_CHUNK == 0
    depth = norm1_g.shape[0]
    xp = x_prompt.reshape(b * t, d)
    xs = x_sample.reshape(nb, d)
    h_a, h_c = d // (2 * HEAD), d // HEAD
    outs = {name: [] for name in ("ka_p", "va_p", "ka_s", "va_s", "sb_p", "sb_s",
                                  "kc_p", "vc_p", "lc_p", "kc_s", "vc_s", "lc_s")}
    for i in range(depth):
        j = i // 3
        hp = _rmsnorm(xp, norm1_g[i], BF16)
        hs = _rmsnorm(xs, norm1_g[i], F32)
        kind = i % 3
        if kind == 0:
            lams = (a_lam_q1[j], a_lam_k1[j], a_lam_q2[j], a_lam_k2[j])
            xp, xs, k1, v1, k2, v2 = _layer_a(hp, hs, xp, xs, cache_k_a, cache_v_a, j, page_table, a_w_in,
                                              a_q_norm[j], a_k_norm[j], lams, a_subln_g[j], a_w_out, i, b, t)
            outs["ka_p"].append(k1.reshape(b, t, 2 * h_a, HEAD))
            outs["va_p"].append(v1.reshape(b, t, h_a, 2 * HEAD))
            outs["ka_s"].append(k2.reshape(nb, 1, 2 * h_a, HEAD))
            outs["va_s"].append(v2.reshape(nb, 1, h_a, 2 * HEAD))
        elif kind == 1:
            xp, xs, s1, s2 = _layer_b(hp, hs, xp, xs, state_s_b, j, b_w_in, b_lower_bounds,
                                      b_out_norm_g[j], b_w_out, i, b, t)
            outs["sb_p"].append(s1)
            outs["sb_s"].append(s2)
        else:
            xp, xs, k1, v1, l1, k2, v2, l2 = _layer_c(hp, hs, xp, xs, cache_k_c, cache_v_c, cache_logf_c, j,
                                                      page_table, c_w_in, c_b_f[j], c_q_norm[j],
                                                      c_k_norm[j], c_w_out, b, t)
            outs["kc_p"].append(k1.reshape(b, t, h_c, HEAD))
            outs["vc_p"].append(v1.reshape(b, t, h_c, HEAD))
            outs["lc_p"].append(l1.reshape(b, t, h_c))
            outs["kc_s"].append(k2.reshape(nb, 1, h_c, HEAD))
            outs["vc_s"].append(v2.reshape(nb, 1, h_c, HEAD))
            outs["lc_s"].append(l2.reshape(nb, 1, h_c))
        xp, xs = _ffn(xp, xs, norm2_g[i], ffn_w_gu, ffn_w_down, i)
    st = lambda name: jnp.stack(outs[name])
    return (xp.reshape(b, t, d), xs.reshape(nb, 1, d), st("ka_p"), st("va_p"), st("ka_s"), st("va_s"),
            st("sb_p"), st("sb_s"), st("kc_p"), st("vc_p"), st("lc_p"), st("kc_s"), st("vc_s"), st("lc_s"))
```

```python
import functools
import math

import numpy as np
import jax
import jax.numpy as jnp
from jax import lax
from jax.experimental import pallas as pl
from jax.experimental.pallas import tpu as pltpu

F32 = jnp.float32
BF16 = jnp.bfloat16
EPS = 1e-6
LANES = 128
SUBLANES = 8
VMEM_LIMIT = 56 * 1024 * 1024
MM_VMEM_BUDGET = 48 * 1024 * 1024
HEAD = 128
GLA_CHUNK = 128
ATT_TQ_A = 512
ATT_TQ_C = 256
ATT_HEADS_C = 4
ATT_CK_A = 512
ATT_CK_C = 256
DECODE_PAGES = 8
GLA_HEADS = 16
LOG2E = 1.4426950408889634
NEG_INF = float("-inf")


def _cp(n_axes):
    return pltpu.CompilerParams(dimension_semantics=("arbitrary",) * n_axes,
                                vmem_limit_bytes=VMEM_LIMIT)


def _sigmoid(x):
    return 1.0 / (1.0 + jnp.exp(-x))


def _silu(x):
    return x * _sigmoid(x)


def _log_sigmoid(x):
    return jnp.minimum(x, 0.0) - jnp.log1p(jnp.exp(-jnp.abs(x)))


def _nt_dot(a, b):
    return lax.dot_general(a, b, (((1,), (1,)), ((), ())), preferred_element_type=F32)


def _rmsnorm_kernel(x_ref, g_ref, o_ref, *, post_scale):
    x = x_ref[...].astype(F32)
    ms = jnp.mean(x * x, axis=-1, keepdims=True)
    y = x * lax.rsqrt(ms + EPS) * g_ref[...]
    if post_scale != 1.0:
        y = y * post_scale
    o_ref[...] = y.astype(o_ref.dtype)


def _rmsnorm(x, g, out_dtype, post_scale=1.0):
    rows, d = x.shape
    tr = min(rows, 1024)
    return pl.pallas_call(
        functools.partial(_rmsnorm_kernel, post_scale=post_scale),
        grid=(rows // tr,),
        in_specs=[pl.BlockSpec((tr, d), lambda i: (i, 0)),
                  pl.BlockSpec((1, d), lambda i: (0, 0))],
        out_specs=pl.BlockSpec((tr, d), lambda i: (i, 0)),
        out_shape=jax.ShapeDtypeStruct((rows, d), out_dtype),
        compiler_params=_cp(1),
        name="rmsnorm",
    )(x, g.reshape(1, d).astype(F32))


def _mm_epilogue(acc, up, aux, out_refs, mode, tn):
    if mode == "swiglu":
        acc = _silu(acc) * up
    elif mode == "res":
        acc = aux + acc
    elif mode == "logsig":
        acc = _log_sigmoid(acc + aux)
    if mode == "gnorm":
        for c in range(tn // HEAD):
            blk = acc[:, c * HEAD:(c + 1) * HEAD]
            ms = jnp.mean(blk * blk, axis=-1, keepdims=True)
            y = blk * lax.rsqrt(ms + EPS) * aux
            for o_ref in out_refs:
                o_ref[:, c * HEAD:(c + 1) * HEAD] = y.astype(o_ref.dtype)
    else:
        for o_ref in out_refs:
            o_ref[...] = acc.astype(o_ref.dtype)


def _mm_kernel(*refs, mode, n_out, n_out_s, tn):
    dual = mode == "swiglu"
    has_aux = mode in ("gnorm", "res", "logsig")
    it = iter(refs)
    x_ref = next(it)
    w_ref = next(it)
    w2_ref = next(it) if dual else None
    aux_ref = next(it) if has_aux else None
    xs_ref = next(it)
    aux_s_ref = next(it) if mode == "res" else aux_ref
    out_refs = [next(it) for _ in range(n_out)]
    out_s_refs = [next(it) for _ in range(n_out_s)]
    wbf_ref = next(it)
    wbf2_ref = next(it) if dual else None

    @pl.when(pl.program_id(1) == 0)
    def _():
        wbf_ref[...] = w_ref[...].astype(BF16)
        if dual:
            wbf2_ref[...] = w2_ref[...].astype(BF16)
        xs = xs_ref[...].astype(BF16)
        acc_s = jnp.dot(xs, wbf_ref[...], preferred_element_type=F32)
        up_s = jnp.dot(xs, wbf2_ref[...], preferred_element_type=F32) if dual else None
        _mm_epilogue(acc_s, up_s, aux_s_ref[...] if has_aux else None, out_s_refs, mode, tn)

    x = x_ref[...].astype(BF16)
    acc = jnp.dot(x, wbf_ref[...], preferred_element_type=F32)
    up = jnp.dot(x, wbf2_ref[...], preferred_element_type=F32) if dual else None
    _mm_epilogue(acc, up, aux_ref[...] if has_aux else None, out_refs, mode, tn)


def _matmul(x, xs, w, *, layer=0, col0=0, ncols=None, mode="plain", out_dtypes=(F32,), out_dtypes_s=(F32,),
            aux=None, aux_s=None, col1=None, tn=1024):
    if w.ndim == 2:
        w = w[None]
    m, k = x.shape
    ms = xs.shape[0]
    ncols = w.shape[2] - col0 if ncols is None else ncols
    tn = min(tn, ncols)
    n_w = 2 if mode == "swiglu" else 1

    def vmem_estimate(rows):
        blocks = rows * k * x.dtype.itemsize + n_w * k * tn * 4
        blocks += sum(rows * tn * jnp.dtype(dt).itemsize for dt in out_dtypes)
        blocks += rows * tn * 4 if mode == "res" else 0
        return 2 * blocks + n_w * k * tn * 2

    tm = next((r for r in (1024, 512, 256) if m % r == 0 and vmem_estimate(r) <= MM_VMEM_BUDGET), min(m, 128))
    assert m % tm == 0 and ncols % tn == 0 and col0 % tn == 0
    nb0 = col0 // tn
    in_specs = [pl.BlockSpec((tm, k), lambda n, i: (i, 0)),
                pl.BlockSpec((None, k, tn), lambda n, i: (layer, 0, n + nb0))]
    args = [x, w]
    scratch = [pltpu.VMEM((k, tn), BF16)]
    if mode == "swiglu":
        assert col1 % tn == 0
        nb1 = col1 // tn
        in_specs.append(pl.BlockSpec((None, k, tn), lambda n, i: (layer, 0, n + nb1)))
        args.append(w)
        scratch.append(pltpu.VMEM((k, tn), BF16))
    if mode == "gnorm":
        in_specs.append(pl.BlockSpec((1, HEAD), lambda n, i: (0, 0)))
        args.append(aux.reshape(1, HEAD).astype(F32))
    elif mode == "res":
        in_specs.append(pl.BlockSpec((tm, tn), lambda n, i: (i, n)))
        args.append(aux)
    elif mode == "logsig":
        in_specs.append(pl.BlockSpec((1, tn), lambda n, i: (0, n)))
        args.append(aux.reshape(1, ncols).astype(F32))
    in_specs.append(pl.BlockSpec((ms, k), lambda n, i: (0, 0)))
    args.append(xs)
    if mode == "res":
        in_specs.append(pl.BlockSpec((ms, tn), lambda n, i: (0, n)))
        args.append(aux_s)
    outs = pl.pallas_call(
        functools.partial(_mm_kernel, mode=mode, n_out=len(out_dtypes), n_out_s=len(out_dtypes_s), tn=tn),
        grid=(ncols // tn, m // tm),
        in_specs=in_specs,
        out_specs=([pl.BlockSpec((tm, tn), lambda n, i: (i, n)) for _ in out_dtypes]
                   + [pl.BlockSpec((ms, tn), lambda n, i: (0, n)) for _ in out_dtypes_s]),
        out_shape=([jax.ShapeDtypeStruct((m, ncols), dt) for dt in out_dtypes]
                   + [jax.ShapeDtypeStruct((ms, ncols), dt) for dt in out_dtypes_s]),
        scratch_shapes=scratch,
        compiler_params=_cp(2),
        name="mm_" + mode,
    )(*args)
    return tuple(outs[:len(out_dtypes)]), tuple(outs[len(out_dtypes):])


def _lam_value(lq1, lk1, lq2, lk2, lam_init):
    a = jnp.sum(lq1[...] * lk1[...], axis=-1, keepdims=True)
    b = jnp.sum(lq2[...] * lk2[...], axis=-1, keepdims=True)
    return jnp.exp(a) - jnp.exp(b) + lam_init


def _causal_rows(qs, k_ref, v_ref, s_ref, bias_of_chunk, qi, n_chunks, tq, ck, value_cols):
    cs = (HEAD ** -0.5) * LOG2E
    n_maps = len(qs)
    row = lax.broadcasted_iota(jnp.int32, (tq, ck), 0)
    col = lax.broadcasted_iota(jnp.int32, (tq, ck), 1)
    m_part = [jnp.full((tq, LANES), NEG_INF, F32) for _ in range(n_maps)]
    for c in range(n_chunks):
        k = k_ref[0, c * ck:(c + 1) * ck, :]
        for j in range(n_maps):
            s = _nt_dot(qs[j], k[:, j * HEAD:(j + 1) * HEAD]) * cs + bias_of_chunk(c, j)
            if c == n_chunks - 1:
                s = jnp.where(row - col >= c * ck - qi * tq, s, NEG_INF)
            s_ref[j, :, c * ck:(c + 1) * ck] = s
            for g in range(ck // LANES):
                m_part[j] = jnp.maximum(m_part[j], s[:, g * LANES:(g + 1) * LANES])

    outs = []
    for j in range(n_maps):
        m = jnp.max(m_part[j], axis=1, keepdims=True)
        l_part = jnp.zeros((tq, LANES), F32)
        o = None
        for c in range(n_chunks):
            p = jnp.exp2(s_ref[j, :, c * ck:(c + 1) * ck] - m)
            for g in range(ck // LANES):
                l_part = l_part + p[:, g * LANES:(g + 1) * LANES]
            pv = jnp.dot(p.astype(BF16), v_ref[0, c * ck:(c + 1) * ck, value_cols[j]],
                         preferred_element_type=F32)
            o = pv if o is None else o + pv
        outs.append(o * (1.0 / jnp.sum(l_part, axis=1, keepdims=True)))
    return outs


def _for_each_chunk_count(qi, tq, ck, t, fn):
    per = ck // tq
    for n_chunks in range(1, t // ck + 1):
        pl.when(qi // per == n_chunks - 1)(functools.partial(fn, n_chunks))


def _attn_a_kernel(q_ref, k_ref, v_ref, lq1, lk1, lq2, lk2, sg_ref, o_ref, s_ref,
                   *, tq, ck, t, n_heads, lam_init):
    h = pl.program_id(1)
    qi = pl.program_id(2)
    slope2 = LOG2E * jnp.exp2(-8.0 * (jnp.full((1, 1), h, jnp.int32).astype(F32) + 1.0) / n_heads)
    lam = _lam_value(lq1, lk1, lq2, lk2, lam_init)
    col = lax.broadcasted_iota(jnp.int32, (1, ck), 1)
    q = q_ref[0]
    qs = [q[:, j * HEAD:(j + 1) * HEAD] for j in range(2)]

    def run(n_chunks):
        bias = lambda c, j: slope2 * (col + (c * ck - qi * tq)).astype(F32)
        both = slice(0, 2 * HEAD)
        o1, o2 = _causal_rows(qs, k_ref, v_ref, s_ref, bias, qi, n_chunks, tq, ck, (both, both))
        o = o1 - lam * o2
        ms =jnp.mean(o * o, axis=-1, keepdims=True)
        y = o * lax.rsqrt(ms + EPS) * sg_ref[...]
        o_ref[0] = (y * (1.0 - lam_init)).astype(o_ref.dtype)

    _for_each_chunk_count(qi, tq, ck, t, run)


def _attn_tiles(t, tq, ck):
    tq = min(tq, t)
    ck = min(ck, t)
    assert t % ck == 0 and ck % tq == 0
    return tq, ck


def _attn_a_prompt(q, k, v, lq1, lk1, lq2, lk2, subln_g, lam_init):
    b, t, d = q.shape
    n_heads = d // (2 * HEAD)
    tq, ck = _attn_tiles(t, ATT_TQ_A, ATT_CK_A)
    vec = lambda a: a.reshape(1, -1).astype(F32)
    small = lambda w: pl.BlockSpec((1, w), lambda bi, h, qi: (0, 0))
    return pl.pallas_call(
        functools.partial(_attn_a_kernel, tq=tq, ck=ck, t=t, n_heads=n_heads, lam_init=lam_init),
        grid=(b, n_heads, t // tq),
        in_specs=[pl.BlockSpec((1, tq, 2 * HEAD), lambda bi, h, qi: (bi, qi, h)),
                  pl.BlockSpec((1, t, 2 * HEAD), lambda bi, h, qi: (bi, 0, h)),
                  pl.BlockSpec((1, t, 2 * HEAD), lambda bi, h, qi: (bi, 0, h)),
                  small(HEAD), small(HEAD), small(HEAD), small(HEAD), small(2 * HEAD)],
        out_specs=pl.BlockSpec((1, tq, 2 * HEAD), lambda bi, h, qi: (bi, qi, h)),
        out_shape=jax.ShapeDtypeStruct((b, t, d), BF16),
        scratch_shapes=[pltpu.VMEM((2, tq, t), F32)],
        compiler_params=_cp(3),
        name="attn_a_prompt",
    )(q, k, v, vec(lq1), vec(lk1), vec(lq2), vec(lk2), vec(subln_g))


def _attn_c_kernel(q_ref, k_ref, v_ref, f_ref, o_ref, s_ref, *, tq, ck, t, hg):
    qi = pl.program_id(2)
    q_start = pl.multiple_of(qi * tq, tq)
    f_first = [f_ref[0, j, :, pl.ds(q_start, tq)][:, 0:1] for j in range(hg)]
    cols = [slice(j * HEAD, (j + 1) * HEAD) for j in range(hg)]
    q = q_ref[0]

    def run(n_chunks):
        bias = lambda c, j: (f_first[j] - f_ref[0, j, :, c * ck:(c + 1) * ck]) * LOG2E
        outs = _causal_rows([q[:, cj] for cj in cols], k_ref, v_ref, s_ref, bias, qi, n_chunks, tq, ck, cols)
        for cj, o in zip(cols, outs):
            o_ref[0, :, cj] = o.astype(o_ref.dtype)

    _for_each_chunk_count(qi, tq, ck, t, run)


def _attn_c_prompt(q, k, v, f_t):
    b, t, d = q.shape
    n_heads = d // HEAD
    hg = math.gcd(ATT_HEADS_C, n_heads)
    tq, ck = _attn_tiles(t, ATT_TQ_C, ATT_CK_C)
    return pl.pallas_call(
        functools.partial(_attn_c_kernel, tq=tq, ck=ck, t=t, hg=hg),
        grid=(b, n_heads // hg, t // tq),
        in_specs=[pl.BlockSpec((1, tq, hg * HEAD), lambda bi, h, qi: (bi, qi, h)),
                  pl.BlockSpec((1, t, hg * HEAD), lambda bi, h, qi: (bi, 0, h)),
                  pl.BlockSpec((1, t, hg * HEAD), lambda bi, h, qi: (bi, 0, h)),
                  pl.BlockSpec((1, hg, 1, t), lambda bi, h, qi: (bi, h, 0, 0))],
        out_specs=pl.BlockSpec((1, tq, hg * HEAD), lambda bi, h, qi: (bi, qi, h)),
        out_shape=jax.ShapeDtypeStruct((b, t, d), BF16),
        scratch_shapes=[pltpu.VMEM((hg, tq, t), F32)],
        compiler_params=_cp(3),
        name="attn_c_prompt",
    )(q, k, v, f_t)


def _cumsum_rows(x, n):
    row = lax.broadcasted_iota(jnp.int32, x.shape, 0)
    d = 1
    while d < n:
        x = x + jnp.where(row >= d, pltpu.roll(x, d, 0), 0.0)
        d *= 2
    return x


def _logf_cumsum_kernel(x_ref, o_ref, carry_ref, *, tc, n_heads):
    @pl.when(pl.program_id(1) == 0)
    def _():
        carry_ref[...] = jnp.zeros(carry_ref.shape, F32)

    f = _cumsum_rows(x_ref[0], tc) + carry_ref[...]
    carry_ref[...] = f[tc - 1:tc, :]
    o_ref[0] = f.T[0:n_heads, :]


def _logf_cumsum_t(logf_pad, n_heads, tc=512):
    b, t, _ = logf_pad.shape
    tc = min(tc, t)
    return pl.pallas_call(
        functools.partial(_logf_cumsum_kernel, tc=tc, n_heads=n_heads),
        grid=(b, t // tc),
        in_specs=[pl.BlockSpec((1, tc, LANES), lambda bi, c: (bi, c, 0))],
        out_specs=pl.BlockSpec((1, n_heads, tc), lambda bi, c: (bi, 0, c)),
        out_shape=jax.ShapeDtypeStruct((b, n_heads, t), F32),
        scratch_shapes=[pltpu.VMEM((1, LANES), F32)],
        compiler_params=_cp(2),
        name="logf_cumsum",
    )(logf_pad)


def _gla_levels():
    return [GLA_CHUNK >> (i + 1) for i in range(int(math.log2(GLA_CHUNK)))]


def _gla_masks():
    t = np.arange(GLA_CHUNK)[:, None]
    s = np.arange(GLA_CHUNK)[None, :]
    masks = []
    for b in _gla_levels():
        masks.append((t // (2 * b) == s // (2 * b)) & (t % (2 * b) >= b) & (s % (2 * b) < b))
    masks.append(t == s)
    return jnp.asarray(np.stack(masks).astype(np.float32))


def _level_reference(g_ref, hh, b):
    row = lambda r, n: jnp.broadcast_to(g_ref[hh, r:r + 1, :], (n, LANES))
    if 2 * b >= SUBLANES:
        return jnp.concatenate([row(2 * b * j + b - 1, 2 * b) for j in range(GLA_CHUNK // (2 * b))], axis=0)
    sub = lax.broadcasted_iota(jnp.int32, (SUBLANES, LANES), 0)
    pieces = []
    for j in range(GLA_CHUNK // SUBLANES):
        base = SUBLANES * j
        n_blk = SUBLANES // (2 * b)
        piece = row(base + (n_blk - 1) * 2 * b + b - 1, SUBLANES)
        for i in range(n_blk - 2, -1, -1):
            piece = jnp.where(sub < (i + 1) * 2 * b, row(base + i * 2 * b + b - 1, SUBLANES), piece)
        pieces.append(piece)
    return jnp.concatenate(pieces, axis=0)


def _lower_bound(lb_ref, layer_idx):
    x = lb_ref[...].astype(F32)
    e = jnp.exp(x - jnp.max(x, axis=0, keepdims=True))
    p = e / jnp.sum(e, axis=0, keepdims=True)
    r = lax.broadcasted_iota(jnp.int32, p.shape, 0)
    return jnp.sum(jnp.where((r >= 1) & (r <= layer_idx), p, 0.0), axis=0, keepdims=True)


def _gla_kernel(q_ref, f_ref, i_ref, g_ref, lb_ref, gn_ref, mask_ref, o_ref, s_ref,
                st_ref, gs_ref, *, layer_idx, n_chunks, hg):
    c = pl.program_id(2)
    scale = HEAD ** -0.5

    @pl.when(c == 0)
    def _():
        st_ref[...] = jnp.zeros(st_ref.shape, F32)

    lb_all = _lower_bound(lb_ref, layer_idx)
    levels = _gla_levels()
    for hh in range(hg):
        cols = slice(hh * HEAD, (hh + 1) * HEAD)
        lb = lb_all[:, cols]
        q = _silu(q_ref[:, cols]) * scale
        gate = lb + (1.0 - lb) * _sigmoid(f_ref[:, cols])
        kk = 1.0 - gate
        v = i_ref[:, cols]
        v_bf = v.astype(BF16)
        g_cum = _cumsum_rows(jnp.log(gate), GLA_CHUNK)
        gs_ref[hh] = g_cum
        st = st_ref[hh]
        o = _nt_dot((q * jnp.exp(g_cum)).astype(BF16), st.astype(BF16))

        a = jnp.sum(q * kk, axis=1, keepdims=True) * mask_ref[len(levels)]
        for lvl, b in enumerate(levels):
            e = jnp.exp(-jnp.abs(g_cum - _level_reference(gs_ref, hh, b)))
            qs = (q * e).astype(BF16)
            ks = (kk * e).astype(BF16)
            a = a + _nt_dot(qs, ks) * mask_ref[lvl]
        o = o + jnp.dot(a.astype(BF16), v_bf, preferred_element_type=F32)

        g_last = gs_ref[hh, GLA_CHUNK - 1:GLA_CHUNK, :]
        k_dec = (kk * jnp.exp(g_last - g_cum)).astype(BF16)
        st_new = st * jnp.exp(g_last) + jnp.dot(v.T.astype(BF16), k_dec, preferred_element_type=F32)
        st_ref[hh] = st_new

        ms = jnp.mean(o * o, axis=-1, keepdims=True)
        y = o * lax.rsqrt(ms + EPS) * gn_ref[...]
        o_ref[:, cols] = (y * _silu(g_ref[:, cols])).astype(o_ref.dtype)

    @pl.when(c == n_chunks - 1)
    def _():
        for hh in range(hg):
            s_ref[hh] = st_ref[hh].T


def _gla_prompt(proj, lower_bounds, g_norm, layer_idx, b, t):
    m, d4 = proj.shape
    d = d4 // 4
    n_heads = d // HEAD
    hg = math.gcd(GLA_HEADS, n_heads)
    n_groups = n_heads // hg
    depth = lower_bounds.shape[0]
    n_chunks = t // GLA_CHUNK
    masks = _gla_masks()
    blk = lambda off: pl.BlockSpec((GLA_CHUNK, hg * HEAD),
                                   lambda bi, h, c: (bi * n_chunks + c, off * n_groups + h))
    return pl.pallas_call(
        functools.partial(_gla_kernel, layer_idx=layer_idx, n_chunks=n_chunks, hg=hg),
        grid=(b, n_groups, n_chunks),
        in_specs=[blk(0), blk(1), blk(2), blk(3),
                  pl.BlockSpec((depth, hg * HEAD), lambda bi, h, c: (0, h)),
                  pl.BlockSpec((1, HEAD), lambda bi, h, c: (0, 0)),
                  pl.BlockSpec(masks.shape, lambda bi, h, c: (0, 0, 0))],
        out_specs=[pl.BlockSpec((GLA_CHUNK, hg * HEAD), lambda bi, h, c: (bi * n_chunks + c, h)),
                   pl.BlockSpec((None, hg, HEAD, HEAD), lambda bi, h, c: (bi, h, 0, 0))],
        out_shape=[jax.ShapeDtypeStruct((m, d), BF16),
                   jax.ShapeDtypeStruct((b, n_heads, HEAD, HEAD), F32)],
        scratch_shapes=[pltpu.VMEM((hg, HEAD, HEAD), F32), pltpu.VMEM((hg, GLA_CHUNK, HEAD), F32)],
        compiler_params=_cp(3),
        name="gla_prompt",
    )(proj, proj, proj, proj, lower_bounds, g_norm.reshape(1, HEAD).astype(F32), masks)


def _gla_step_kernel(i_ref, g_ref, qt_ref, ft_ref, lbt_ref, gn_ref, s_in_ref, o_ref, s_out_ref,
                     *, layer_idx, n_batch):
    scale = HEAD ** -0.5
    xt = lbt_ref[...].astype(F32)
    et = jnp.exp(xt - jnp.max(xt, axis=1, keepdims=True))
    pt = et / jnp.sum(et, axis=1, keepdims=True)
    ct = lax.broadcasted_iota(jnp.int32, pt.shape, 1)
    lb_col = jnp.sum(jnp.where((ct >= 1) & (ct <= layer_idx), pt, 0.0), axis=1, keepdims=True)

    q_col = _silu(qt_ref[...]) * scale
    gate_col = lb_col + (1.0 - lb_col) * _sigmoid(ft_ref[...])
    k_col = 1.0 - gate_col
    a_all = jnp.sum(q_col * k_col, axis=0, keepdims=True)
    qg_col = q_col * gate_col
    for bi in range(n_batch):
        s_old = s_in_ref[bi]
        v_row = i_ref[bi:bi + 1, :]
        o = (jnp.sum(qg_col[:, bi:bi + 1] * s_old, axis=0, keepdims=True)
             + a_all[:, bi:bi + 1] * v_row)
        s_out_ref[bi] = gate_col[:, bi:bi + 1] * s_old + k_col[:, bi:bi + 1] * v_row
        ms = jnp.mean(o * o, axis=-1, keepdims=True)
        y = o * lax.rsqrt(ms + EPS) * gn_ref[...]
        o_ref[bi:bi + 1, :] = y * _silu(g_ref[bi:bi + 1, :])


def _gla_step(proj, state, layer_j, lower_bounds, g_norm, layer_idx):
    nb, d4 = proj.shape
    d = d4 // 4
    n_heads = d // HEAD
    depth = lower_bounds.shape[0]
    proj_t = proj.T
    lb_t = lower_bounds.T
    row = lambda off: pl.BlockSpec((nb, HEAD), lambda h: (0, off * n_heads + h))
    col = lambda off: pl.BlockSpec((HEAD, nb), lambda h: (off * n_heads + h, 0))
    return pl.pallas_call(
        functools.partial(_gla_step_kernel, layer_idx=layer_idx, n_batch=nb),
        grid=(n_heads,),
        in_specs=[row(2), row(3), col(0), col(1),
                  pl.BlockSpec((HEAD, depth), lambda h: (h, 0)),
                  pl.BlockSpec((1, HEAD), lambda h: (0, 0)),
                  pl.BlockSpec((None, nb, None, HEAD, HEAD), lambda h: (layer_j, 0, h, 0, 0))],
        out_specs=[pl.BlockSpec((nb, HEAD), lambda h: (0, h)),
                   pl.BlockSpec((nb, None, HEAD, HEAD), lambda h: (0, h, 0, 0))],
        out_shape=[jax.ShapeDtypeStruct((nb, d), F32),
                   jax.ShapeDtypeStruct((nb, n_heads, HEAD, HEAD), F32)],
        compiler_params=_cp(1),
        name="gla_step",
    )(proj, proj, proj_t, proj_t, lb_t, g_norm.reshape(1, HEAD).astype(F32), state)


def _decode_kernel(pt_ref, *refs, kind, n_pages, page, n_rows, per_step, lam_init):
    it = iter(refs)
    q_ref = next(it)
    kn_ref = next(it)
    vn_ref = next(it)
    kc_refs = [next(it) for _ in range(per_step)]
    vc_refs = [next(it) for _ in range(per_step)]
    if kind == "c":
        lfc_refs = [next(it) for _ in range(per_step)]
        lfn_ref = next(it)
    else:
        lq1, lk1, lq2, lk2 = next(it), next(it), next(it), next(it)
    o_ref = next(it)
    qbf_ref, mask_ref, m_ref, l_ref, acc_ref = next(it), next(it), next(it), next(it), next(it)
    if kind == "c":
        carry_ref = next(it)
    else:
        pair_ref = next(it)

    p = pl.program_id(1)
    scale = HEAD ** -0.5
    group = n_rows if kind == "c" else n_rows // 2
    ncols = page * group
    rows_col = lax.broadcasted_iota(jnp.int32, (n_rows, 1), 0)

    @pl.when(p == 0)
    def _():
        qbf_ref[...] = q_ref[0].astype(BF16)
        r = lax.broadcasted_iota(jnp.int32, (n_rows, ncols), 0)
        cidx = lax.broadcasted_iota(jnp.int32, (n_rows, ncols), 1)
        head_of_row = r if kind == "c" else r // 2
        mask_ref[...] = jnp.where(cidx % group == head_of_row, 0.0, NEG_INF)
        m_ref[...] = jnp.full(m_ref.shape, NEG_INF, F32)
        l_ref[...] = jnp.zeros(l_ref.shape, F32)
        acc_ref[...] = jnp.zeros(acc_ref.shape, F32)
        if kind == "c":
            carry_ref[...] = lfn_ref[0]

    lane = lax.broadcasted_iota(jnp.int32, (1, ncols), 1)
    qbf = qbf_ref[...]
    mask = mask_ref[...]
    if kind == "a":
        slope = jnp.exp2(-8.0 * ((rows_col // 2).astype(F32) + 1.0) / group)
    else:
        carry = carry_ref[...]
    scores, values = [], []
    for g in range(per_step):
        page_idx = n_pages - 1 - (p * per_step + g)
        kc_ref, vc_ref = kc_refs[g], vc_refs[g]
        if kind == "a":
            s_even = _nt_dot(qbf, kc_ref[pl.ds(0, ncols, stride=2), :].astype(BF16))
            s_odd = _nt_dot(qbf, kc_ref[pl.ds(1, ncols, stride=2), :].astype(BF16))
            s = jnp.where(rows_col % 2 == 0, s_even, s_odd) * scale
            dist = (n_pages * page - (page_idx * page + lane // group)).astype(F32)
            s = s - slope * dist + mask
            values.append(vc_ref[...].reshape(ncols, vc_ref.shape[-1]))
        else:
            s = _nt_dot(qbf, kc_ref[...].astype(BF16)) * scale
            lf2d = lfc_refs[g][...]
            flat = jnp.concatenate([lf2d[i:i + 1, :] for i in range(lf2d.shape[0])], axis=1)
            y = flat
            tot = flat
            dd = group
            while dd < ncols:
                y = y + jnp.where(lane + dd < ncols, pltpu.roll(y, ncols - dd, 1), 0.0)
                tot = tot + pltpu.roll(tot, dd, 1)
                dd *= 2
            s = s + (y - flat + carry) + mask
            carry = carry + tot
            values.append(vc_ref[...])
        scores.append(s)
    if kind == "c":
        carry_ref[...] = carry

    m_prev = m_ref[...]
    m_new = m_prev
    for s in scores:
        m_new = jnp.maximum(m_new, jnp.max(s, axis=1, keepdims=True))
    alpha = jnp.exp(m_prev - m_new)
    l_new = alpha * l_ref[...]
    acc = alpha * acc_ref[...]
    for s, v2d in zip(scores, values):
        pr = jnp.exp(s - m_new)
        l_new = l_new + jnp.sum(pr, axis=1, keepdims=True)
        acc = acc + jnp.dot(pr.astype(BF16), v2d.astype(BF16), preferred_element_type=F32)
    l_ref[...] = l_new
    acc_ref[...] = acc
    m_ref[...] = m_new

    @pl.when(p == n_pages // per_step - 1)
    def _():
        s_new = jnp.sum(q_ref[0] * kn_ref[0], axis=1, keepdims=True) * scale
        m_prev = m_ref[...]
        m_fin = jnp.maximum(m_prev, s_new)
        alpha = jnp.exp(m_prev - m_fin)
        p_new = jnp.exp(s_new - m_fin)
        l_fin = alpha * l_ref[...] + p_new
        acc = alpha * acc_ref[...] + p_new * vn_ref[0]
        o_full = acc * (1.0 / l_fin)
        if kind == "a":
            lam = _lam_value(lq1, lk1, lq2, lk2, lam_init)
            signed = o_full * jnp.where(rows_col % 2 == 0, 1.0, -lam)
            for c in range(pair_ref.shape[0]):
                pair_ref[c] = signed[:, c * LANES:(c + 1) * LANES]
                o_ref[0, :, c * LANES:(c + 1) * LANES] = (pair_ref[c, pl.ds(0, group, stride=2), :]
                                                          + pair_ref[c, pl.ds(1, group, stride=2), :])
        else:
            o_ref[0] = o_full


def _decode_attention(kind, q, k_new, v_new, cache_k, cache_v, layer_j, page_table, extras, lam_init=0.0):
    nb, n_rows, _ = q.shape
    vw = v_new.shape[-1]
    n_pages = page_table.shape[1]
    page = cache_k.shape[2] // n_rows
    group = n_rows if kind == "c" else n_rows // 2
    ncols = page * group
    per_step = math.gcd(DECODE_PAGES, n_pages)
    per_b = lambda shape: pl.BlockSpec((1,) + shape, lambda b, p, pt: (b,) + (0,) * len(shape))

    def paged(shape, g):
        return pl.BlockSpec((None, None) + shape,
                            lambda b, p, pt: (layer_j, pt[b, n_pages - 1 - (p * per_step + g)]) + (0,) * len(shape))

    steps = range(per_step)
    in_specs = ([per_b((n_rows, HEAD)), per_b((n_rows, HEAD)), per_b((n_rows, vw))]
                + [paged((page * n_rows, HEAD), g) for g in steps]
                + [paged(cache_v.shape[2:], g) for g in steps])
    args = [q, k_new, v_new] + [cache_k] * per_step + [cache_v] * per_step
    scratch = [pltpu.VMEM((n_rows, HEAD), BF16), pltpu.VMEM((n_rows, ncols), F32),
               pltpu.VMEM((n_rows, 1), F32), pltpu.VMEM((n_rows, 1), F32), pltpu.VMEM((n_rows, vw), F32)]
    if kind == "c":
        cache_logf, logf_new = extras
        in_specs += [paged(cache_logf.shape[2:], g) for g in steps] + [per_b((1, ncols))]
        args += [cache_logf] * per_step + [logf_new]
        scratch += [pltpu.VMEM((1, ncols), F32)]
    else:
        in_specs += [pl.BlockSpec((1, HEAD), lambda b, p, pt: (0, 0))] * 4
        args += [e.reshape(1, HEAD).astype(F32) for e in extras]
        scratch += [pltpu.VMEM((vw // LANES, n_rows, LANES), F32)]
    return pl.pallas_call(
        functools.partial(_decode_kernel, kind=kind, n_pages=n_pages, page=page, n_rows=n_rows,
                          per_step=per_step, lam_init=lam_init),
        grid_spec=pltpu.PrefetchScalarGridSpec(
            num_scalar_prefetch=1,
            grid=(nb, n_pages // per_step),
            in_specs=in_specs,
            out_specs=pl.BlockSpec((1, group, vw), lambda b, p, pt: (b, 0, 0)),
            scratch_shapes=scratch),
        out_shape=jax.ShapeDtypeStruct((nb, group, vw), F32),
        compiler_params=_cp(2),
        name="decode_" + kind,
    )(page_table, *args)


def _layer_a(hp, hs, xp, xs, cache_k, cache_v, j, page_table, w_in, q_norm, k_norm, lams, subln_g, w_out,
             layer_idx, b, t):
    d = xp.shape[1]
    lam_init = 0.8 - 0.6 * math.exp(-0.3 * layer_idx)
    n_layers, n_phys, page, n_rows, _ = cache_k.shape
    ck = cache_k.reshape(n_layers, n_phys, page * n_rows, HEAD)

    nb = hs.shape[0]
    (q,), (qs,) = _matmul(hp, hs, w_in, layer=j, col0=0, ncols=d, mode="gnorm", aux=q_norm, out_dtypes=(BF16,))
    (k32, k16), (ks,) = _matmul(hp, hs, w_in, layer=j, col0=d, ncols=d, mode="gnorm", aux=k_norm,
                                out_dtypes=(F32, BF16))
    (v32, v16), (vs,) = _matmul(hp, hs, w_in, layer=j, col0=2 * d, ncols=d, out_dtypes=(F32, BF16))
    o = _attn_a_prompt(q.reshape(b, t, d), k16.reshape(b, t, d), v16.reshape(b, t, d), *lams, subln_g, lam_init)
    v_rows = jnp.repeat(vs.reshape(nb, n_rows // 2, 2 * HEAD), 2, axis=1)
    os_ = _decode_attention("a", qs.reshape(nb, n_rows, HEAD), ks.reshape(nb, n_rows, HEAD), v_rows,
                            ck, cache_v, j, page_table, lams, lam_init)
    os_ = _rmsnorm(os_.reshape(nb * n_rows // 2, 2 * HEAD), subln_g, F32, post_scale=1.0 - lam_init)
    (xp,), (xs,) = _matmul(o.reshape(b * t, d), os_.reshape(nb, d), w_out, layer=j, mode="res", aux=xp, aux_s=xs)
    return xp, xs, k32, v32, ks, vs


def _layer_b(hp, hs, xp, xs, state, j, w_in, lower_bounds, g_norm, w_out, layer_idx, b, t):
    (proj,), (proj_s,) = _matmul(hp, hs, w_in, layer=j)
    o, s_p = _gla_prompt(proj, lower_bounds, g_norm, layer_idx, b, t)
    o_s, s_s = _gla_step(proj_s, state, j, lower_bounds, g_norm, layer_idx)
    (xp,), (xs,) = _matmul(o, o_s, w_out, layer=j, mode="res", aux=xp, aux_s=xs)
    return xp, xs, s_p, s_s


def _layer_c(hp, hs, xp, xs, cache_k, cache_v, cache_logf, j, page_table, w_in, b_f, q_norm, k_norm, w_out, b, t):
    d = xp.shape[1]
    n_heads = d // HEAD
    n_layers, n_phys, page = cache_k.shape[:3]
    ck = cache_k.reshape(n_layers, n_phys, page * n_heads, HEAD)
    cv = cache_v.reshape(n_layers, n_phys, page * n_heads, HEAD)
    clf = cache_logf.reshape(n_layers, n_phys, page * n_heads // LANES, LANES)
    w_f = jnp.pad(w_in[j, :, 3 * d:], ((0, 0), (0, LANES - n_heads)))
    b_pad = jnp.pad(b_f.astype(F32), (0, LANES - n_heads))

    nb = hs.shape[0]
    (q,), (qs,) = _matmul(hp, hs, w_in, layer=j, col0=0, ncols=d, mode="gnorm", aux=q_norm, out_dtypes=(BF16,))
    (k32, k16), (ks,) = _matmul(hp, hs, w_in, layer=j, col0=d, ncols=d, mode="gnorm", aux=k_norm,
                                out_dtypes=(F32, BF16))
    (v32, v16), (vs,) = _matmul(hp, hs, w_in, layer=j, col0=2 * d, ncols=d, out_dtypes=(F32, BF16))
    (logf,), (lfs_pad,) = _matmul(hp, hs, w_f, mode="logsig", aux=b_pad)
    lfs = lfs_pad[:, :n_heads]
    f_t = _logf_cumsum_t(logf.reshape(b, t, LANES), n_heads)
    o = _attn_c_prompt(q.reshape(b, t, d), k16.reshape(b, t, d), v16.reshape(b, t, d),
                       f_t.reshape(b, n_heads, 1, t))
    lf_new = jnp.tile(lfs, (1, page)).reshape(nb, 1, page * n_heads)
    os_ = _decode_attention("c", qs.reshape(nb, n_heads, HEAD), ks.reshape(nb, n_heads, HEAD),
                            vs.reshape(nb, n_heads, HEAD), ck, cv, j, page_table, (clf, lf_new))
    (xp,), (xs,) = _matmul(o.reshape(b * t, d), os_.reshape(nb, d), w_out, layer=j, mode="res", aux=xp, aux_s=xs)
    return xp, xs, k32, v32, logf[:, :n_heads], ks, vs, lfs


def _ffn(xp, xs, g, w_gu, w_down, layer):
    d_ff = w_down.shape[1]
    hp = _rmsnorm(xp, g, BF16)
    hs = _rmsnorm(xs, g, F32)
    (act,), (act_s,) = _matmul(hp, hs, w_gu, layer=layer, col0=0, ncols=d_ff, col1=d_ff, mode="swiglu",
                               out_dtypes=(BF16,), out_dtypes_s=(BF16,), tn=512)
    (xp,), (xs,) = _matmul(act, act_s, w_down, layer=layer, mode="res", aux=xp, aux_s=xs, tn=512)
    return xp, xs


def kernel(x_prompt, x_sample, cache_k_a, cache_v_a, state_s_b, cache_k_c, cache_v_c, cache_logf_c, page_table, norm1_g, norm2_g, a_w_in, a_q_norm, a_k_norm, a_lam_q1, a_lam_k1, a_lam_q2, a_lam_k2, a_subln_g, a_w_out, b_w_in, b_lower_bounds, b_out_norm_g, b_w_out, c_w_in, c_b_f, c_q_norm, c_k_norm, c_w_out, ffn_w_gu, ffn_w_down):
    b, t, d = x_prompt.shape
    nb, ds, _ = x_sample.shape
    assert ds == 1 and d % (2 * HEAD) == 0 and t % GLA_CHUNK == 0
    depth = norm1_g.shape[0]
    xp = x_prompt.reshape(b * t, d)
    xs = x_sample.reshape(nb, d)
    h_a, h_c = d // (2 * HEAD), d // HEAD
    outs = {name: [] for name in ("ka_p", "va_p", "ka_s", "va_s", "sb_p", "sb_s",
                                  "kc_p", "vc_p", "lc_p", "kc_s", "vc_s", "lc_s")}
    for i in range(depth):
        j = i // 3
        hp = _rmsnorm(xp, norm1_g[i], BF16)
        hs = _rmsnorm(xs, norm1_g[i], F32)
        kind = i % 3
        if kind == 0:
            lams = (a_lam_q1[j], a_lam_k1[j], a_lam_q2[j], a_lam_k2[j])
            xp, xs, k1, v1, k2, v2 = _layer_a(hp, hs, xp, xs, cache_k_a, cache_v_a, j, page_table, a_w_in,
                                              a_q_norm[j], a_k_norm[j], lams, a_subln_g[j], a_w_out, i, b, t)
            outs["ka_p"].append(k1.reshape(b, t, 2 * h_a, HEAD))
            outs["va_p"].append(v1.reshape(b, t, h_a, 2 * HEAD))
            outs["ka_s"].append(k2.reshape(nb, 1, 2 * h_a, HEAD))
            outs["va_s"].append(v2.reshape(nb, 1, h_a, 2 * HEAD))
        elif kind == 1:
            xp, xs, s1, s2 = _layer_b(hp, hs, xp, xs, state_s_b, j, b_w_in, b_lower_bounds,
                                      b_out_norm_g[j], b_w_out, i, b, t)
            outs["sb_p"].append(s1)
            outs["sb_s"].append(s2)
        else:
            xp, xs, k1, v1, l1, k2, v2, l2 = _layer_c(hp, hs, xp, xs, cache_k_c, cache_v_c, cache_logf_c, j,
                                                      page_table, c_w_in, c_b_f[j], c_q_norm[j],
                                                      c_k_norm[j], c_w_out, b, t)
            outs["kc_p"].append(k1.reshape(b, t, h_c, HEAD))
            outs["vc_p"].append(v1.reshape(b, t, h_c, HEAD))
            outs["lc_p"].append(l1.reshape(b, t, h_c))
            outs["kc_s"].append(k2.reshape(nb, 1, h_c, HEAD))
            outs["vc_s"].append(v2.reshape(nb, 1, h_c, HEAD))
            outs["lc_s"].append(l2.reshape(nb, 1, h_c))
        xp, xs = _ffn(xp, xs, norm2_g[i], ffn_w_gu, ffn_w_down, i)
    st = lambda name: jnp.stack(outs[name])
    return (xp.reshape(b, t, d), xs.reshape(nb, 1, d), st("ka_p"), st("va_p"), st("ka_s"), st("va_s"),
            st("sb_p"), st("sb_s"), st("kc_p"), st("vc_p"), st("lc_p"), st("kc_s"), st("vc_s"), st("lc_s"))
```
